```python
import math
import jax
import jax.numpy as jnp
from jax import lax
import numpy as np

D_MODEL = 1024
BATCH = 2
SEQ = 8192
DEPTH = 1

D_A = 1024
G_A = 8
C_A = D_A // G_A
CHUNK = 128
N_HEADS_B = 8
HD_B = 128
D_B = N_HEADS_B * HD_B
IDX_HEADS = 8
IDX_DIM = 64
INDEX_TOPK_MAX = 256
Q_BLOCK = 128
D_FF = 2816
CONV_W = 3
ALPHA = (2.0 * DEPTH) ** 0.25
BETA = (8.0 * DEPTH) ** -0.25
LN_EPS = 1e-5

SPLITS = (D_A, D_A, D_B, D_B, D_B, IDX_HEADS * IDX_DIM, IDX_DIM, IDX_HEADS, D_MODEL, D_MODEL)
D_IN_PROJ = sum(SPLITS)
SPLIT_POINTS = tuple(int(s) for s in np.cumsum(SPLITS)[:-1])

kernel_name = "hybrid_gmlp_dsa_convffn_deepnorm"


def layer_norm(x, g, b):
    xf = x.astype(jnp.float32)
    mu = jnp.mean(xf, axis=-1, keepdims=True)
    var = jnp.mean(jnp.square(xf - mu), axis=-1, keepdims=True)
    y = (xf - mu) * lax.rsqrt(var + LN_EPS) * g.astype(jnp.float32) + b.astype(jnp.float32)
    return y.astype(x.dtype)


def spatial_gating(u, v, w_s, b_s):
    B, L, _ = u.shape
    nc = L // CHUNK
    vb = v.reshape(B, nc, CHUNK, G_A, C_A)
    causal = jnp.tril(jnp.ones((CHUNK, CHUNK), dtype=bool))
    w = jnp.where(causal[None], w_s, 0)
    mixed = jnp.einsum('gts,bnsgc->bntgc', w, vb) + jnp.transpose(b_s)[None, None, :, :, None]
    return u * mixed.reshape(B, L, D_A)


def dsa_attention(q, k, v, q_idx, k_idx, w_idx):
    B, L, H, Dh = q.shape
    topk = min(INDEX_TOPK_MAX, L // 4)
    nb = L // Q_BLOCK
    scale = 1.0 / math.sqrt(Dh)
    pos = jnp.arange(L, dtype=jnp.int32)
    k_f = k_idx.astype(jnp.float32)

    def to_blocks(a):
        return jnp.moveaxis(a.reshape((B, nb, Q_BLOCK) + a.shape[2:]), 1, 0)

    def block(args):
        qb, qib, wb, tpos = args
        rel = jax.nn.relu(jnp.einsum('bthd,bsd->bths', qib.astype(jnp.float32), k_f))
        score = jnp.einsum('bth,bths->bts', wb.astype(jnp.float32), rel)
        causal = pos[None, None, :] <= tpos[None, :, None]
        score = jnp.where(causal, score, -jnp.inf)
        _, idx = lax.top_k(score, topk)
        k_sel = jax.vmap(lambda kb, ib: kb[ib])(k, idx)
        v_sel = jax.vmap(lambda vb, ib: vb[ib])(v, idx)
        logits = jnp.einsum('bthd,btkhd->bthk', qb, k_sel).astype(jnp.float32) * scale
        valid = (idx <= tpos[None, :, None])[:, :, None, :]
        p = jax.nn.softmax(jnp.where(valid, logits, -jnp.inf), axis=-1)
        return jnp.einsum('bthk,btkhd->bthd', p.astype(v.dtype), v_sel)

    out = lax.map(block, (to_blocks(q), to_blocks(q_idx), to_blocks(w_idx),
                          pos.reshape(nb, Q_BLOCK)))
    return jnp.moveaxis(out, 0, 1).reshape(B, L, H * Dh)


def causal_dwconv(h, w, b):
    L = h.shape[1]
    hp = jnp.pad(h, ((0, 0), (CONV_W - 1, 0), (0, 0)))
    return b + sum(hp[:, j:j + L, :] * w[j] for j in range(CONV_W))


def setup_inputs(seed: int = 0) -> dict:
    key = jax.random.key(seed)
    ks = jax.random.split(key, 20)

    def nrm(k, shape, scale):
        return jax.random.normal(k, shape, jnp.float32) * scale

    return {
        "x": nrm(ks[0], (BATCH, SEQ, D_MODEL), 1.0),
        "w_in": nrm(ks[1], (DEPTH, D_MODEL, D_IN_PROJ), D_MODEL ** -0.5),
        "ln_v_g": 1.0 + nrm(ks[2], (DEPTH, D_A), 0.02),
        "ln_v_b": nrm(ks[3], (DEPTH, D_A), 0.02),
        "w_spatial": nrm(ks[4], (DEPTH, G_A, CHUNK, CHUNK), CHUNK ** -0.5),
        "b_spatial": 1.0 + nrm(ks[5], (DEPTH, G_A, CHUNK), 0.1),
        "ln_kidx_g": 1.0 + nrm(ks[6], (DEPTH, IDX_DIM), 0.02),
        "ln_kidx_b": nrm(ks[7], (DEPTH, IDX_DIM), 0.02),
        "w_branch_a": nrm(ks[8], (DEPTH, D_A, D_MODEL), D_A ** -0.5),
        "w_branch_b": nrm(ks[9], (DEPTH, D_B, D_MODEL), D_B ** -0.5),
        "w_o": nrm(ks[10], (DEPTH, D_MODEL, D_MODEL), BETA * D_MODEL ** -0.5),
        "ln1_g": 1.0 + nrm(ks[11], (DEPTH, D_MODEL), 0.02),
        "ln1_b": nrm(ks[12], (DEPTH, D_MODEL), 0.02),
        "w_up": nrm(ks[13], (DEPTH, D_MODEL, 2 * D_FF), D_MODEL ** -0.5),
        "conv_w": nrm(ks[14], (DEPTH, CONV_W, 2 * D_FF), CONV_W ** -0.5),
        "conv_b": nrm(ks[15], (DEPTH, 2 * D_FF), 0.02),
        "w_down": nrm(ks[16], (DEPTH, D_FF, D_MODEL), BETA * D_FF ** -0.5),
        "ln2_g": 1.0 + nrm(ks[17], (DEPTH, D_MODEL), 0.02),
        "ln2_b": nrm(ks[18], (DEPTH, D_MODEL), 0.02),
    }


def reference(x, w_in, ln_v_g, ln_v_b, w_spatial, b_spatial, ln_kidx_g, ln_kidx_b,
              w_branch_a, w_branch_b, w_o, ln1_g, ln1_b, w_up, conv_w, conv_b,
              w_down, ln2_g, ln2_b):
    B, L, _ = x.shape
    h = x
    for i in range(DEPTH):
        proj = h @ w_in[i]
        u, v, q, k, vv, q_idx, k_idx, w_idx, g_a, g_b = jnp.split(proj, SPLIT_POINTS, axis=-1)

        u = jax.nn.gelu(u)
        v = layer_norm(jax.nn.gelu(v), ln_v_g[i], ln_v_b[i])
        a_out = spatial_gating(u, v, w_spatial[i], b_spatial[i])

        q = q.reshape(B, L, N_HEADS_B, HD_B)
        k = k.reshape(B, L, N_HEADS_B, HD_B)
        vv = vv.reshape(B, L, N_HEADS_B, HD_B)
        q_idx = q_idx.reshape(B, L, IDX_HEADS, IDX_DIM)
        k_idx = layer_norm(k_idx, ln_kidx_g[i], ln_kidx_b[i])
        w_idx = w_idx * (IDX_HEADS ** -0.5 * IDX_DIM ** -0.5)
        b_out = dsa_attention(q, k, vv, q_idx, k_idx, w_idx)

        mix = jax.nn.sigmoid(g_a) * (a_out @ w_branch_a[i]) + jax.nn.sigmoid(g_b) * (b_out @ w_branch_b[i])
        y = mix @ w_o[i]
        h = layer_norm(ALPHA * h + y, ln1_g[i], ln1_b[i])

        up = causal_dwconv(h @ w_up[i], conv_w[i], conv_b[i])
        gate, val = jnp.split(up, 2, axis=-1)
        f = (jax.nn.silu(gate) * val) @ w_down[i]
        h = layer_norm(ALPHA * h + f, ln2_g[i], ln2_b[i])
    return h
```

```python
import functools
import math

import jax
import jax.numpy as jnp
from jax import lax
from jax.experimental import pallas as pl
from jax.experimental.pallas import tpu as pltpu

F32 = jnp.float32
BF16 = jnp.bfloat16
I32 = jnp.int32

D_MODEL = 1024
D_A = 1024
G_A = 8
CHUNK = 128
N_HEADS = 8
HEAD_DIM = 128
IDX_HEADS = 8
IDX_DIM = 64
IDX_PAD = 128
TOPK_MAX = 256
D_FF = 2816
CONV_W = 3
ALPHA = 2.0 ** 0.25
LN_EPS = 1e-5

INT_MIN = -(2 ** 31)
NEG_BIG = -1e30
F32_LOWEST = -3.4028234663852886e38

PROJ_TM = 256
IDX_TQ = 256
IDX_SC = 512
ATT_TQ = 256
ATT_TK = 512
POST_TM = 512
FFN_TM = 512
FFN_TF = 256
HALO = 16

VMEM_LIMIT = 56 * 1024 * 1024

NT_DIMS = (((1,), (1,)), ((), ()))


def _dot(a, b):
    return jnp.dot(a, b, preferred_element_type=F32)


def _dot_nt(a, b):
    return lax.dot_general(a, b, NT_DIMS, preferred_element_type=F32)


def _layer_norm(x, g, b):
    mu = jnp.mean(x, axis=-1, keepdims=True)
    d = x - mu
    var = jnp.mean(d * d, axis=-1, keepdims=True)
    return d * lax.rsqrt(var + LN_EPS) * g + b


def _const_spec(shape):
    nd = len(shape)
    return pl.BlockSpec(shape, lambda *_: (0,) * nd, pipeline_mode=pl.Buffered(1))


def _proj_kernel(x_ref, wu_ref, wv_ref, wq_ref, wk_ref, wvt_ref, wi_ref, wwt_ref, wga_ref,
                 wgb_ref, wba_ref, ws_ref, bs_ref, lnvg_ref, lnvb_ref, lnkg_ref, lnkb_ref,
                 mixa_ref, q_ref, k_ref, vt_ref, qi_ref, ki_ref, wit_ref, sgb_ref, a_scr):
    tm = x_ref.shape[0]
    xb = x_ref[...].astype(BF16)

    u = jax.nn.gelu(_dot(xb, wu_ref[...]))
    v = _layer_norm(jax.nn.gelu(_dot(xb, wv_ref[...])), lnvg_ref[...], lnvb_ref[...]).astype(BF16)
    for c in range(tm // CHUNK):
        rows = slice(c * CHUNK, (c + 1) * CHUNK)
        for g in range(G_A):
            cols = slice(g * CHUNK, (g + 1) * CHUNK)
            mixed = _dot(ws_ref[g], v[rows, cols]) + bs_ref[g]
            a_scr[rows, cols] = (u[rows, cols] * mixed).astype(BF16)
    gate_a = jax.nn.sigmoid(_dot(xb, wga_ref[...]))
    mixa_ref[...] = gate_a * _dot(a_scr[...], wba_ref[...])

    q_ref[...] = _dot(xb, wq_ref[...]).astype(BF16)
    k_ref[...] = _dot(xb, wk_ref[...]).astype(BF16)
    vt_ref[0] = _dot_nt(wvt_ref[...], xb).astype(BF16)

    r = _dot(xb, wi_ref[...])
    qi_ref[...] = r[:, :IDX_HEADS * IDX_PAD].astype(BF16)
    kraw = r[:, IDX_HEADS * IDX_PAD:]
    lane = lax.broadcasted_iota(I32, kraw.shape, 1)
    valid = lane < IDX_DIM
    mu = jnp.sum(kraw, axis=1, keepdims=True) * (1.0 / IDX_DIM)
    d = jnp.where(valid, kraw - mu, 0.0)
    var = jnp.sum(d * d, axis=1, keepdims=True) * (1.0 / IDX_DIM)
    ki_ref[...] = (d * lax.rsqrt(var + LN_EPS) * lnkg_ref[...] + lnkb_ref[...]).astype(BF16)
    wt = _dot_nt(wwt_ref[...], xb)
    wit_ref[0] = wt[:IDX_HEADS, :] * (IDX_HEADS ** -0.5 * IDX_DIM ** -0.5)

    sgb_ref[...] = jax.nn.sigmoid(_dot(xb, wgb_ref[...]))


def _run_proj(x2, w, B, L):
    M = x2.shape[0]
    tm = PROJ_TM
    tiles_per_seq = L // tm
    row = lambda i: (i, 0)
    seq = lambda i: (i // tiles_per_seq, 0, i % tiles_per_seq)
    in_specs = [pl.BlockSpec((tm, D_MODEL), row)] + [_const_spec(a.shape) for a in w]
    out_shape = (
        jax.ShapeDtypeStruct((M, D_MODEL), F32),
        jax.ShapeDtypeStruct((M, D_MODEL), BF16),
        jax.ShapeDtypeStruct((M, D_MODEL), BF16),
        jax.ShapeDtypeStruct((B, D_MODEL, L), BF16),
        jax.ShapeDtypeStruct((M, IDX_HEADS * IDX_PAD), BF16),
        jax.ShapeDtypeStruct((M, IDX_PAD), BF16),
        jax.ShapeDtypeStruct((B, IDX_HEADS, L), F32),
        jax.ShapeDtypeStruct((M, D_MODEL), F32),
    )
    out_specs = (
        pl.BlockSpec((tm, D_MODEL), row),
        pl.BlockSpec((tm, D_MODEL), row),
        pl.BlockSpec((tm, D_MODEL), row),
        pl.BlockSpec((1, D_MODEL, tm), seq),
        pl.BlockSpec((tm, IDX_HEADS * IDX_PAD), row),
        pl.BlockSpec((tm, IDX_PAD), row),
        pl.BlockSpec((1, IDX_HEADS, tm), seq),
        pl.BlockSpec((tm, D_MODEL), row),
    )
    return pl.pallas_call(
        _proj_kernel,
        grid=(M // tm,),
        in_specs=in_specs,
        out_specs=out_specs,
        out_shape=out_shape,
        scratch_shapes=[pltpu.VMEM((tm, D_A), BF16)],
        compiler_params=pltpu.CompilerParams(
            dimension_semantics=("arbitrary",), vmem_limit_bytes=VMEM_LIMIT),
        name="proj",
    )(x2, *w)


def _sum_sublane_tiles(m):
    n = m.shape[0] // 8
    parts = [m[r * 8:(r + 1) * 8, :] for r in range(n)]
    accs = parts[:4]
    for r in range(4, n):
        accs[r % 4] = accs[r % 4] + parts[r]
    return (accs[0] + accs[1]) + (accs[2] + accs[3])


def _index_kernel(q_ref, k_ref, w_ref, bias_ref, score_scr, cut_scr, *, topk, n_chunks_total):
    tq = q_ref.shape[1]
    sc = IDX_SC
    i = pl.program_id(1)
    n_chunks = ((i + 1) * tq + sc - 1) // sc
    t_idx = i * tq + lax.broadcasted_iota(I32, (1, tq), 1)
    row_iota = lax.broadcasted_iota(I32, (sc, tq), 0)
    int_max = 2 ** 31 - 1

    def chunk_start(j):
        return pl.multiple_of(j * sc, sc)

    def score_body(j, carry):
        kc = k_ref[0, pl.ds(chunk_start(j), sc), :]
        acc = jnp.zeros((sc, tq), F32)
        for h in range(IDX_HEADS):
            qh = q_ref[0, :, h * IDX_PAD:(h + 1) * IDX_PAD]
            acc = acc + w_ref[0, h:h + 1, :] * jnp.maximum(_dot_nt(kc, qh), 0.0)
        score_scr[pl.ds(chunk_start(j), sc), :] = jnp.where(row_iota + j * sc <= t_idx, acc, F32_LOWEST)
        return carry

    lax.fori_loop(0, n_chunks, score_body, 0)

    def count(pred):
        def body(j, acc):
            blk = score_scr[pl.ds(chunk_start(j), sc), :]
            return acc + _sum_sublane_tiles(jnp.where(pred(blk, j), 1, 0))
        acc = lax.fori_loop(0, n_chunks, body, jnp.zeros((8, tq), I32))
        return jnp.sum(acc, axis=0, keepdims=True)

    def key_to_float(key):
        return lax.bitcast_convert_type(key ^ ((key >> 31) & 0x7FFFFFFF), F32)

    def bit_body(b, cand):
        trial = cand ^ jnp.left_shift(jnp.int32(1), 31 - b)
        trial_f = key_to_float(trial)
        c = count(lambda blk, j: blk >= trial_f)
        return jnp.where(c >= topk, trial, cand)

    thr = key_to_float(lax.fori_loop(0, 32, bit_body, jnp.full((1, tq), INT_MIN, I32)))

    n_gt = count(lambda blk, j: blk > thr)
    n_ge = count(lambda blk, j: blk >= thr)
    n_take = topk - n_gt
    cut_scr[...] = jnp.full(cut_scr.shape, int_max, I32)
    need_cut = jnp.max(jnp.where(n_ge - n_gt > n_take, 1, 0)) > 0

    @pl.when(need_cut)
    def _():
        nbits = max(1, (n_chunks_total * sc - 1).bit_length())

        def tie_body(b, lo):
            cand = lo + jnp.left_shift(jnp.int32(1), nbits - 1 - b)
            f = count(lambda blk, j: jnp.where(blk == thr, row_iota + j * sc, int_max) <= cand)
            return jnp.where(f < n_take, cand, lo)

        lo = lax.fori_loop(0, nbits, tie_body, jnp.full((1, tq), -1, I32))
        cut_scr[...] = jnp.broadcast_to(lo + 1, cut_scr.shape)

    cut = cut_scr[0:1, :]

    def mask_body(j, carry):
        blk = score_scr[pl.ds(chunk_start(j), sc), :]
        s_idx = row_iota + j * sc
        tie_ok = (blk == thr) & (s_idx <= cut)
        sel = ((blk > thr) | tie_ok) & (s_idx <= t_idx)
        bias_ref[0, pl.ds(chunk_start(j), sc), :] = jnp.where(sel, 0.0, NEG_BIG).astype(BF16)
        return carry

    lax.fori_loop(0, n_chunks, mask_body, 0)

    def fill_body(j, carry):
        bias_ref[0, pl.ds(chunk_start(j), sc), :] = jnp.full((sc, tq), NEG_BIG, BF16)
        return carry

    lax.fori_loop(n_chunks, n_chunks_total, fill_body, 0)


def _run_index(q_idx, k_idx, w_idx_t, B, L, topk):
    tq = min(IDX_TQ, L)
    kernel = functools.partial(_index_kernel, topk=topk, n_chunks_total=L // IDX_SC)
    return pl.pallas_call(
        kernel,
        grid=(B, L // tq),
        in_specs=[
            pl.BlockSpec((1, tq, IDX_HEADS * IDX_PAD), lambda b, i: (b, i, 0)),
            pl.BlockSpec((1, L, IDX_PAD), lambda b, i: (b, 0, 0)),
            pl.BlockSpec((1, IDX_HEADS, tq), lambda b, i: (b, 0, i)),
        ],
        out_specs=pl.BlockSpec((1, L, tq), lambda b, i: (b, 0, i)),
        out_shape=jax.ShapeDtypeStruct((B, L, L), BF16),
        scratch_shapes=[pltpu.VMEM((L, tq), F32), pltpu.VMEM((8, tq), I32)],
        compiler_params=pltpu.CompilerParams(
            dimension_semantics=("arbitrary", "arbitrary"), vmem_limit_bytes=VMEM_LIMIT),
        name="indexer",
    )(q_idx, k_idx, w_idx_t)


def _attn_kernel(q_ref, k_ref, vt_ref, bias_ref, o_ref, m_scr, l_scr, acc_scr):
    tq = q_ref.shape[1]
    tk = k_ref.shape[1]
    i = pl.program_id(1)
    j = pl.program_id(2)
    n_kblocks = ((i + 1) * tq + tk - 1) // tk
    scale = 1.0 / math.sqrt(HEAD_DIM)

    @pl.when(j == 0)
    def _():
        m_scr[...] = jnp.full(m_scr.shape, NEG_BIG, F32)
        l_scr[...] = jnp.zeros(l_scr.shape, F32)
        acc_scr[...] = jnp.zeros(acc_scr.shape, F32)

    @pl.when(j < n_kblocks)
    def _():
        bias = bias_ref[0].astype(F32)
        for h in range(N_HEADS):
            hs = slice(h * HEAD_DIM, (h + 1) * HEAD_DIM)
            s = _dot_nt(k_ref[0, :, hs], q_ref[0, :, hs]) * scale + bias
            m_prev = m_scr[h:h + 1, :]
            m_new = jnp.maximum(m_prev, jnp.max(s, axis=0, keepdims=True))
            alpha = jnp.exp(m_prev - m_new)
            p = jnp.exp(s - m_new)
            l_scr[h:h + 1, :] = alpha * l_scr[h:h + 1, :] + jnp.sum(p, axis=0, keepdims=True)
            acc_scr[hs, :] = alpha * acc_scr[hs, :] + _dot(vt_ref[0, hs, :], p.astype(BF16))
            m_scr[h:h + 1, :] = m_new

    @pl.when(j == n_kblocks - 1)
    def _():
        for h in range(N_HEADS):
            hs = slice(h * HEAD_DIM, (h + 1) * HEAD_DIM)
            out_t = acc_scr[hs, :] / l_scr[h:h + 1, :]
            o_ref[0, :, hs] = out_t.T.astype(BF16)


def _run_attn(q, k, v_t, bias, B, L):
    tq = min(ATT_TQ, L)
    tk = min(ATT_TK, L)

    def kblock(i, j):
        return jnp.minimum(j, ((i + 1) * tq + tk - 1) // tk - 1)

    return pl.pallas_call(
        _attn_kernel,
        grid=(B, L // tq, L // tk),
        in_specs=[
            pl.BlockSpec((1, tq, D_MODEL), lambda b, i, j: (b, i, 0)),
            pl.BlockSpec((1, tk, D_MODEL), lambda b, i, j: (b, kblock(i, j), 0)),
            pl.BlockSpec((1, D_MODEL, tk), lambda b, i, j: (b, 0, kblock(i, j))),
            pl.BlockSpec((1, tk, tq), lambda b, i, j: (b, kblock(i, j), i)),
        ],
        out_specs=pl.BlockSpec((1, tq, D_MODEL), lambda b, i, j: (b, i, 0)),
        out_shape=jax.ShapeDtypeStruct((B, L, D_MODEL), BF16),
        scratch_shapes=[
            pltpu.VMEM((N_HEADS, tq), F32),
            pltpu.VMEM((N_HEADS, tq), F32),
            pltpu.VMEM((D_MODEL, tq), F32),
        ],
        compiler_params=pltpu.CompilerParams(
            dimension_semantics=("arbitrary", "arbitrary", "arbitrary"),
            vmem_limit_bytes=VMEM_LIMIT),
        name="attention",
    )(q, k, v_t, bias)


def _post_kernel(mixa_ref, sgb_ref, bo_ref, x_ref, wbb_ref, wo_ref, g_ref, b_ref, h_ref):
    mix = mixa_ref[...] + sgb_ref[...] * _dot(bo_ref[...], wbb_ref[...])
    y = _dot(mix.astype(BF16), wo_ref[...])
    h_ref[...] = _layer_norm(ALPHA * x_ref[...] + y, g_ref[...], b_ref[...])


def _run_post(mixa, sgb, b_out, x2, wbb, wo, g, b):
    M = x2.shape[0]
    tm = POST_TM
    row = lambda i: (i, 0)
    tile = pl.BlockSpec((tm, D_MODEL), row)
    return pl.pallas_call(
        _post_kernel,
        grid=(M // tm,),
        in_specs=[tile, tile, tile, tile, _const_spec(wbb.shape), _const_spec(wo.shape),
                  _const_spec(g.shape), _const_spec(b.shape)],
        out_specs=tile,
        out_shape=jax.ShapeDtypeStruct((M, D_MODEL), F32),
        compiler_params=pltpu.CompilerParams(
            dimension_semantics=("arbitrary",), vmem_limit_bytes=VMEM_LIMIT),
        name="post",
    )(mixa, sgb, b_out, x2, wbb, wo, g, b)


def _ffn_kernel(h_ref, halo_ref, wup_ref, cw_ref, cb_ref, wdn_ref, g_ref, b_ref, o_ref,
                *, tiles_per_seq):
    tm = h_ref.shape[0]
    i = pl.program_id(0)
    h = h_ref[...]
    keep = (i % tiles_per_seq != 0).astype(F32)
    xe = jnp.concatenate([halo_ref[...] * keep, h], axis=0).astype(BF16)

    def conv(up, col0):
        cols = slice(col0, col0 + FFN_TF)
        w = cw_ref[:, cols]
        y = cb_ref[:, cols] + w[2:3, :] * up[HALO:, :]
        y = y + w[1:2, :] * up[HALO - 1:HALO - 1 + tm, :]
        return y + w[0:1, :] * up[HALO - 2:HALO - 2 + tm, :]

    acc = jnp.zeros((tm, D_MODEL), F32)
    for c in range(D_FF // FFN_TF):
        g0 = c * FFN_TF
        v0 = D_FF + c * FFN_TF
        gate = conv(_dot(xe, wup_ref[:, g0:g0 + FFN_TF]), g0)
        val = conv(_dot(xe, wup_ref[:, v0:v0 + FFN_TF]), v0)
        act = (gate * jax.nn.sigmoid(gate) * val).astype(BF16)
        acc = acc + _dot(act, wdn_ref[g0:g0 + FFN_TF, :])
    o_ref[...] = _layer_norm(ALPHA * h + acc, g_ref[...], b_ref[...])


def _run_ffn(h1, wup, cw, cb, wdn, g, b, L):
    M = h1.shape[0]
    tm = min(FFN_TM, L)
    kernel = functools.partial(_ffn_kernel, tiles_per_seq=L // tm)
    halo_blocks = tm // HALO
    return pl.pallas_call(
        kernel,
        grid=(M // tm,),
        in_specs=[
            pl.BlockSpec((tm, D_MODEL), lambda i: (i, 0)),
            pl.BlockSpec((HALO, D_MODEL), lambda i: (jnp.maximum(i * halo_blocks - 1, 0), 0)),
            _const_spec(wup.shape), _const_spec(cw.shape), _const_spec(cb.shape),
            _const_spec(wdn.shape), _const_spec(g.shape), _const_spec(b.shape),
        ],
        out_specs=pl.BlockSpec((tm, D_MODEL), lambda i: (i, 0)),
        out_shape=jax.ShapeDtypeStruct((M, D_MODEL), F32),
        compiler_params=pltpu.CompilerParams(
            dimension_semantics=("arbitrary",), vmem_limit_bytes=VMEM_LIMIT),
        name="ffn",
    )(h1, h1, wup, cw, cb, wdn, g, b)


def _prep_proj_weights(w_in, ln_v_g, ln_v_b, w_spatial, b_spatial, ln_kidx_g, ln_kidx_b, w_branch_a):
    o = 0
    segs = []
    for n in (D_A, D_A, D_MODEL, D_MODEL, D_MODEL, IDX_HEADS * IDX_DIM, IDX_DIM, IDX_HEADS,
              D_MODEL, D_MODEL):
        segs.append(w_in[:, o:o + n])
        o += n
    wu, wv, wq, wk, wvv, wqi, wki, wwi, wga, wgb = segs
    kin = w_in.shape[0]
    wqi_pad = jnp.pad(wqi.reshape(kin, IDX_HEADS, IDX_DIM),
                      ((0, 0), (0, 0), (0, IDX_PAD - IDX_DIM))).reshape(kin, IDX_HEADS * IDX_PAD)
    wki_pad = jnp.pad(wki, ((0, 0), (0, IDX_PAD - IDX_DIM)))
    wi = jnp.concatenate([wqi_pad, wki_pad], axis=1)
    wwt = jnp.pad(wwi.T, ((0, 16 - IDX_HEADS), (0, 0)))
    causal = jnp.tril(jnp.ones((CHUNK, CHUNK), dtype=bool))
    ws = jnp.where(causal[None], w_spatial, 0)
    bs = jnp.broadcast_to(b_spatial[:, :, None], (G_A, CHUNK, D_A // G_A))
    pad_k = lambda a: jnp.pad(a, (0, IDX_PAD - IDX_DIM)).reshape(1, IDX_PAD)
    bf = lambda a: a.astype(BF16)
    return (bf(wu), bf(wv), bf(wq), bf(wk), bf(wvv.T), bf(wi), bf(wwt), bf(wga), bf(wgb),
            bf(w_branch_a), bf(ws), bs.astype(F32),
            ln_v_g.reshape(1, -1), ln_v_b.reshape(1, -1), pad_k(ln_kidx_g), pad_k(ln_kidx_b))


def _layer(x, w_in, ln_v_g, ln_v_b, w_spatial, b_spatial, ln_kidx_g, ln_kidx_b, w_branch_a,
           w_branch_b, w_o, ln1_g, ln1_b, w_up, conv_w, conv_b, w_down, ln2_g, ln2_b):
    B, L, D = x.shape
    M = B * L
    topk = min(TOPK_MAX, L // 4)
    x2 = x.reshape(M, D)
    pw = _prep_proj_weights(w_in, ln_v_g, ln_v_b, w_spatial, b_spatial, ln_kidx_g, ln_kidx_b,
                            w_branch_a)
    mixa, q, k, v_t, q_idx, k_idx, w_idx_t, sgb = _run_proj(x2, pw, B, L)
    bias = _run_index(q_idx.reshape(B, L, -1), k_idx.reshape(B, L, -1), w_idx_t, B, L, topk)
    b_out = _run_attn(q.reshape(B, L, D), k.reshape(B, L, D), v_t, bias, B, L)
    h1 = _run_post(mixa, sgb, b_out.reshape(M, D), x2, w_branch_b.astype(BF16), w_o.astype(BF16),
                   ln1_g.reshape(1, -1), ln1_b.reshape(1, -1))
    h2 = _run_ffn(h1, w_up.astype(BF16), conv_w, conv_b.reshape(1, -1), w_down.astype(BF16),
                  ln2_g.reshape(1, -1), ln2_b.reshape(1, -1), L)
    return h2.reshape(B, L, D)


def kernel(x, w_in, ln_v_g, ln_v_b, w_spatial, b_spatial, ln_kidx_g, ln_kidx_b, w_branch_a,
           w_branch_b, w_o, ln1_g, ln1_b, w_up, conv_w, conv_b, w_down, ln2_g, ln2_b):
    h = x
    for i in range(w_in.shape[0]):
        h = _layer(h, w_in[i], ln_v_g[i], ln_v_b[i], w_spatial[i], b_spatial[i], ln_kidx_g[i],
                   ln_kidx_b[i], w_branch_a[i], w_branch_b[i], w_o[i], ln1_g[i], ln1_b[i],
                   w_up[i], conv_w[i], conv_b[i], w_down[i], ln2_g[i], ln2_b[i])
    return h
```

```python
import functools
import math

import jax
import jax.numpy as jnp
from jax import lax
from jax.experimental import pallas as pl
from jax.experimental.pallas import tpu as pltpu

F32 = jnp.float32
BF16 = jnp.bfloat16
I32 = jnp.int32

D_MODEL = 1024
D_A = 1024
G_A = 8
CHUNK = 128
N_HEADS = 8
HEAD_DIM = 128
IDX_HEADS = 8
IDX_DIM = 64
IDX_PAD = 128
TOPK_MAX = 256
D_FF = 2816
CONV_W = 3
ALPHA = 2.0 ** 0.25
LN_EPS = 1e-5

INT_MIN = -(2 ** 31)
NEG_BIG = -1e30
F32_LOWEST = -3.4028234663852886e38

PROJ_TM = 256
IDX_TQ = 256
IDX_SC = 512
ATT_TQ = 256
ATT_TK = 512
POST_TM = 512
FFN_TM = 512
FFN_TF = 256
HALO = 16

VMEM_LIMIT = 56 * 1024 * 1024

NT_DIMS = (((1,), (1,)), ((), ()))


def _dot(a, b):
    return jnp.dot(a, b, preferred_element_type=F32)


def _dot_nt(a, b):
    return lax.dot_general(a, b, NT_DIMS, preferred_element_type=F32)


def _layer_norm(x, g, b):
    mu = jnp.mean(x, axis=-1, keepdims=True)
    d = x - mu
    var = jnp.mean(d * d, axis=-1, keepdims=True)
    return d * lax.rsqrt(var + LN_EPS) * g + b


def _const_spec(shape):
    nd = len(shape)
    return pl.BlockSpec(shape, lambda *_: (0,) * nd, pipeline_mode=pl.Buffered(1))


def _proj_kernel(x_ref, wu_ref, wv_ref, wq_ref, wk_ref, wvt_ref, wi_ref, wwt_ref, wga_ref,
                 wgb_ref, wba_ref, ws_ref, bs_ref, lnvg_ref, lnvb_ref, lnkg_ref, lnkb_ref,
                 mixa_ref, q_ref, k_ref, vt_ref, qi_ref, ki_ref, wit_ref, sgb_ref, a_scr):
    tm = x_ref.shape[0]
    xb = x_ref[...].astype(BF16)

    u = jax.nn.gelu(_dot(xb, wu_ref[...]))
    v = _layer_norm(jax.nn.gelu(_dot(xb, wv_ref[...])), lnvg_ref[...], lnvb_ref[...]).astype(BF16)
    for c in range(tm // CHUNK):
        rows = slice(c * CHUNK, (c + 1) * CHUNK)
        for g in range(G_A):
            cols = slice(g * CHUNK, (g + 1) * CHUNK)
            mixed = _dot(ws_ref[g], v[rows, cols]) + bs_ref[g]
            a_scr[rows, cols] = (u[rows, cols] * mixed).astype(BF16)
    gate_a = jax.nn.sigmoid(_dot(xb, wga_ref[...]))
    mixa_ref[...] = gate_a * _dot(a_scr[...], wba_ref[...])

    q_ref[...] = _dot(xb, wq_ref[...]).astype(BF16)
    k_ref[...] = _dot(xb, wk_ref[...]).astype(BF16)
    vt_ref[0] = _dot_nt(wvt_ref[...], xb).astype(BF16)

    r = _dot(xb, wi_ref[...])
    qi_ref[...] = r[:, :IDX_HEADS * IDX_PAD].astype(BF16)
    kraw = r[:, IDX_HEADS * IDX_PAD:]
    lane = lax.broadcasted_iota(I32, kraw.shape, 1)
    valid = lane < IDX_DIM
    mu = jnp.sum(kraw, axis=1, keepdims=True) * (1.0 / IDX_DIM)
    d = jnp.where(valid, kraw - mu, 0.0)
    var = jnp.sum(d * d, axis=1, keepdims=True) * (1.0 / IDX_DIM)
    ki_ref[...] = (d * lax.rsqrt(var + LN_EPS) * lnkg_ref[...] + lnkb_ref[...]).astype(BF16)
    wt = _dot_nt(wwt_ref[...], xb)
    wit_ref[0] = wt[:IDX_HEADS, :] * (IDX_HEADS ** -0.5 * IDX_DIM ** -0.5)

    sgb_ref[...] = jax.nn.sigmoid(_dot(xb, wgb_ref[...]))


def _run_proj(x2, w, B, L):
    M = x2.shape[0]
    tm = PROJ_TM
    tiles_per_seq = L // tm
    row = lambda i: (i, 0)
    seq = lambda i: (i // tiles_per_seq, 0, i % tiles_per_seq)
    in_specs = [pl.BlockSpec((tm, D_MODEL), row)] + [_const_spec(a.shape) for a in w]
    out_shape = (
        jax.ShapeDtypeStruct((M, D_MODEL), F32),
        jax.ShapeDtypeStruct((M, D_MODEL), BF16),
        jax.ShapeDtypeStruct((M, D_MODEL), BF16),
        jax.ShapeDtypeStruct((B, D_MODEL, L), BF16),
        jax.ShapeDtypeStruct((M, IDX_HEADS * IDX_PAD), BF16),
        jax.ShapeDtypeStruct((M, IDX_PAD), BF16),
        jax.ShapeDtypeStruct((B, IDX_HEADS, L), F32),
        jax.ShapeDtypeStruct((M, D_MODEL), F32),
    )
    out_specs = (
        pl.BlockSpec((tm, D_MODEL), row),
        pl.BlockSpec((tm, D_MODEL), row),
        pl.BlockSpec((tm, D_MODEL), row),
        pl.BlockSpec((1, D_MODEL, tm), seq),
        pl.BlockSpec((tm, IDX_HEADS * IDX_PAD), row),
        pl.BlockSpec((tm, IDX_PAD), row),
        pl.BlockSpec((1, IDX_HEADS, tm), seq),
        pl.BlockSpec((tm, D_MODEL), row),
    )
    return pl.pallas_call(
        _proj_kernel,
        grid=(M // tm,),
        in_specs=in_specs,
        out_specs=out_specs,
        out_shape=out_shape,
        scratch_shapes=[pltpu.VMEM((tm, D_A), BF16)],
        compiler_params=pltpu.CompilerParams(
            dimension_semantics=("arbitrary",), vmem_limit_bytes=VMEM_LIMIT),
        name="proj",
    )(x2, *w)


def _sum_sublane_tiles(m):
    n = m.shape[0] // 8
    parts = [m[r * 8:(r + 1) * 8, :] for r in range(n)]
    accs = parts[:4]
    for r in range(4, n):
        accs[r % 4] = accs[r % 4] + parts[r]
    return (accs[0] + accs[1]) + (accs[2] + accs[3])


def _index_kernel(q_ref, k_ref, w_ref, bias_ref, score_scr, cut_scr, *, topk, n_chunks_total):
    tq = q_ref.shape[1]
    sc = IDX_SC
    i = pl.program_id(1)
    n_chunks = ((i + 1) * tq + sc - 1) // sc
    t_idx = i * tq + lax.broadcasted_iota(I32, (1, tq), 1)
    row_iota = lax.broadcasted_iota(I32, (sc, tq), 0)
    int_max = 2 ** 31 - 1

    def chunk_start(j):
        return pl.multiple_of(j * sc, sc)

    def score_body(j, carry):
        kc = k_ref[0, pl.ds(chunk_start(j), sc), :]
        acc = jnp.zeros((sc, tq), F32)
        for h in range(IDX_HEADS):
            qh = q_ref[0, :, h * IDX_PAD:(h + 1) * IDX_PAD]
            acc = acc + w_ref[0, h:h + 1, :] * jnp.maximum(_dot_nt(kc, qh), 0.0)
        score_scr[pl.ds(chunk_start(j), sc), :] = jnp.where(row_iota + j * sc <= t_idx, acc, F32_LOWEST)
        return carry

    lax.fori_loop(0, n_chunks, score_body, 0)

    def count(pred):
        def body(j, acc):
            blk = score_scr[pl.ds(chunk_start(j), sc), :]
            return acc + _sum_sublane_tiles(jnp.where(pred(blk, j), 1, 0))
        acc = lax.fori_loop(0, n_chunks, body, jnp.zeros((8, tq), I32))
        return jnp.sum(acc, axis=0, keepdims=True)

    def key_to_float(key):
        return lax.bitcast_convert_type(key ^ ((key >> 31) & 0x7FFFFFFF), F32)

    def bit_body(b, cand):
        trial = cand ^ jnp.left_shift(jnp.int32(1), 31 - b)
        trial_f = key_to_float(trial)
        c = count(lambda blk, j: blk >= trial_f)
        return jnp.where(c >= topk, trial, cand)

    thr = key_to_float(lax.fori_loop(0, 32, bit_body, jnp.full((1, tq), INT_MIN, I32)))

    n_gt = count(lambda blk, j: blk > thr)
    n_ge = count(lambda blk, j: blk >= thr)
    n_take = topk - n_gt
    cut_scr[...] = jnp.full(cut_scr.shape, int_max, I32)
    need_cut = jnp.max(jnp.where(n_ge - n_gt > n_take, 1, 0)) > 0

    @pl.when(need_cut)
    def _():
        nbits = max(1, (n_chunks_total * sc - 1).bit_length())

        def tie_body(b, lo):
            cand = lo + jnp.left_shift(jnp.int32(1), nbits - 1 - b)
            f = count(lambda blk, j: jnp.where(blk == thr, row_iota + j * sc, int_max) <= cand)
            return jnp.where(f < n_take, cand, lo)

        lo = lax.fori_loop(0, nbits, tie_body, jnp.full((1, tq), -1, I32))
        cut_scr[...] = jnp.broadcast_to(lo + 1, cut_scr.shape)

    cut = cut_scr[0:1, :]

    def mask_body(j, carry):
        blk = score_scr[pl.ds(chunk_start(j), sc), :]
        s_idx = row_iota + j * sc
        tie_ok = (blk == thr) & (s_idx <= cut)
        sel = ((blk > thr) | tie_ok) & (s_idx <= t_idx)
        bias_ref[0, pl.ds(chunk_start(j), sc), :] = jnp.where(sel, 0.0, NEG_BIG).astype(BF16)
        return carry

    lax.fori_loop(0, n_chunks, mask_body, 0)

    def fill_body(j, carry):
        bias_ref[0, pl.ds(chunk_start(j), sc), :] = jnp.full((sc, tq), NEG_BIG, BF16)
        return carry

    lax.fori_loop(n_chunks, n_chunks_total, fill_body, 0)


def _run_index(q_idx, k_idx, w_idx_t, B, L, topk):
    tq = min(IDX_TQ, L)
    kernel = functools.partial(_index_kernel, topk=topk, n_chunks_total=L // IDX_SC)
    return pl.pallas_call(
        kernel,
        grid=(B, L // tq),
        in_specs=[
            pl.BlockSpec((1, tq, IDX_HEADS * IDX_PAD), lambda b, i: (b, i, 0)),
            pl.BlockSpec((1, L, IDX_PAD), lambda b, i: (b, 0, 0)),
            pl.BlockSpec((1, IDX_HEADS, tq), lambda b, i: (b, 0, i)),
        ],
        out_specs=pl.BlockSpec((1, L, tq), lambda b, i: (b, 0, i)),
        out_shape=jax.ShapeDtypeStruct((B, L, L), BF16),
        scratch_shapes=[pltpu.VMEM((L, tq), F32), pltpu.VMEM((8, tq), I32)],
        compiler_params=pltpu.CompilerParams(
            dimension_semantics=("arbitrary", "arbitrary"), vmem_limit_bytes=VMEM_LIMIT),
        name="indexer",
    )(q_idx, k_idx, w_idx_t)


def _attn_kernel(qblk_ref, kblk_ref, q_ref, k_ref, vt_ref, bias_ref, o_ref, m_scr, l_scr, acc_scr):
    tq = q_ref.shape[1]
    tk = k_ref.shape[1]
    step = pl.program_id(1)
    i = qblk_ref[step]
    j = kblk_ref[step]
    n_kblocks = ((i + 1) * tq + tk - 1) // tk
    log2_scale = math.log2(math.e) / math.sqrt(HEAD_DIM)

    @pl.when(j == 0)
    def _():
        m_scr[...] = jnp.full(m_scr.shape, NEG_BIG, F32)
        l_scr[...] = jnp.zeros(l_scr.shape, F32)
        acc_scr[...] = jnp.zeros(acc_scr.shape, F32)

    bias = bias_ref[0].astype(F32)

    def logits(h):
        hs = slice(h * HEAD_DIM, (h + 1) * HEAD_DIM)
        return _dot_nt(k_ref[0, :, hs], q_ref[0, :, hs])

    raw_next = logits(0)
    for h in range(N_HEADS):
        hs = slice(h * HEAD_DIM, (h + 1) * HEAD_DIM)
        raw = raw_next
        if h + 1 < N_HEADS:
            raw_next = logits(h + 1)
        s = raw * log2_scale + bias
        m_prev = m_scr[h]
        m_new = jnp.maximum(m_prev, jnp.max(s, axis=0, keepdims=True))
        alpha = jnp.exp2(m_prev - m_new)
        p = jnp.exp2(s - m_new[0:1, :])
        l_scr[h] = alpha * l_scr[h] + jnp.sum(p, axis=0, keepdims=True)
        pv = _dot(vt_ref[0, hs, :], p.astype(BF16))
        acc_scr[hs, :] = alpha[0:1, :] * acc_scr[hs, :] + pv
        m_scr[h] = m_new

    @pl.when(j == n_kblocks - 1)
    def _():
        for h in range(N_HEADS):
            hs = slice(h * HEAD_DIM, (h + 1) * HEAD_DIM)
            out_t = acc_scr[hs, :] / l_scr[h][0:1, :]
            o_ref[0, :, hs] = out_t.T.astype(BF16)


def _run_attn(q, k, v_t, bias, B, L):
    tq = min(ATT_TQ, L)
    tk = min(ATT_TK, L)

    pairs = [(i, j) for i in range(L // tq) for j in range(((i + 1) * tq + tk - 1) // tk)]
    qblk = jnp.asarray([p[0] for p in pairs], I32)
    kblk = jnp.asarray([p[1] for p in pairs], I32)

    grid_spec = pltpu.PrefetchScalarGridSpec(
        num_scalar_prefetch=2,
        grid=(B, len(pairs)),
        in_specs=[
            pl.BlockSpec((1, tq, D_MODEL), lambda b, s, qb, kb: (b, qb[s], 0)),
            pl.BlockSpec((1, tk, D_MODEL), lambda b, s, qb, kb: (b, kb[s], 0)),
            pl.BlockSpec((1, D_MODEL, tk), lambda b, s, qb, kb: (b, 0, kb[s])),
            pl.BlockSpec((1, tk, tq), lambda b, s, qb, kb: (b, kb[s], qb[s])),
        ],
        out_specs=pl.BlockSpec((1, tq, D_MODEL), lambda b, s, qb, kb: (b, qb[s], 0)),
        scratch_shapes=[
            pltpu.VMEM((N_HEADS, 8, tq), F32),
            pltpu.VMEM((N_HEADS, 8, tq), F32),
            pltpu.VMEM((D_MODEL, tq), F32),
        ],
    )
    return pl.pallas_call(
        _attn_kernel,
        grid_spec=grid_spec,
        out_shape=jax.ShapeDtypeStruct((B, L, D_MODEL), BF16),
        compiler_params=pltpu.CompilerParams(
            dimension_semantics=("arbitrary", "arbitrary"), vmem_limit_bytes=VMEM_LIMIT),
        name="attention",
    )(qblk, kblk, q, k, v_t, bias)


def _post_kernel(mixa_ref, sgb_ref, bo_ref, x_ref, wbb_ref, wo_ref, g_ref, b_ref, h_ref):
    mix = mixa_ref[...] + sgb_ref[...] * _dot(bo_ref[...], wbb_ref[...])
    y = _dot(mix.astype(BF16), wo_ref[...])
    h_ref[...] = _layer_norm(ALPHA * x_ref[...] + y, g_ref[...], b_ref[...])


def _run_post(mixa, sgb, b_out, x2, wbb, wo, g, b):
    M = x2.shape[0]
    tm = POST_TM
    row = lambda i: (i, 0)
    tile = pl.BlockSpec((tm, D_MODEL), row)
    return pl.pallas_call(
        _post_kernel,
        grid=(M // tm,),
        in_specs=[tile, tile, tile, tile, _const_spec(wbb.shape), _const_spec(wo.shape),
                  _const_spec(g.shape), _const_spec(b.shape)],
        out_specs=tile,
        out_shape=jax.ShapeDtypeStruct((M, D_MODEL), F32),
        compiler_params=pltpu.CompilerParams(
            dimension_semantics=("arbitrary",), vmem_limit_bytes=VMEM_LIMIT),
        name="post",
    )(mixa, sgb, b_out, x2, wbb, wo, g, b)


def _ffn_kernel(h_ref, halo_ref, wup_ref, cw_ref, cb_ref, wdn_ref, g_ref, b_ref, o_ref,
                *, tiles_per_seq):
    tm = h_ref.shape[0]
    i = pl.program_id(0)
    h = h_ref[...]
    keep = (i % tiles_per_seq != 0).astype(F32)
    xe = jnp.concatenate([halo_ref[...] * keep, h], axis=0).astype(BF16)

    def conv(up, col0):
        cols = slice(col0, col0 + FFN_TF)
        w = cw_ref[:, cols]
        y = cb_ref[:, cols] + w[2:3, :] * up[HALO:, :]
        y = y + w[1:2, :] * up[HALO - 1:HALO - 1 + tm, :]
        return y + w[0:1, :] * up[HALO - 2:HALO - 2 + tm, :]

    acc = jnp.zeros((tm, D_MODEL), F32)
    for c in range(D_FF // FFN_TF):
        g0 = c * FFN_TF
        v0 = D_FF + c * FFN_TF
        gate = conv(_dot(xe, wup_ref[:, g0:g0 + FFN_TF]), g0)
        val = conv(_dot(xe, wup_ref[:, v0:v0 + FFN_TF]), v0)
        act = (gate * jax.nn.sigmoid(gate) * val).astype(BF16)
        acc = acc + _dot(act, wdn_ref[g0:g0 + FFN_TF, :])
    o_ref[...] = _layer_norm(ALPHA * h + acc, g_ref[...], b_ref[...])


def _run_ffn(h1, wup, cw, cb, wdn, g, b, L):
    M = h1.shape[0]
    tm = min(FFN_TM, L)
    kernel = functools.partial(_ffn_kernel, tiles_per_seq=L // tm)
    halo_blocks = tm // HALO
    return pl.pallas_call(
        kernel,
        grid=(M // tm,),
        in_specs=[
            pl.BlockSpec((tm, D_MODEL), lambda i: (i, 0)),
            pl.BlockSpec((HALO, D_MODEL), lambda i: (jnp.maximum(i * halo_blocks - 1, 0), 0)),
            _const_spec(wup.shape), _const_spec(cw.shape), _const_spec(cb.shape),
            _const_spec(wdn.shape), _const_spec(g.shape), _const_spec(b.shape),
        ],
        out_specs=pl.BlockSpec((tm, D_MODEL), lambda i: (i, 0)),
        out_shape=jax.ShapeDtypeStruct((M, D_MODEL), F32),
        compiler_params=pltpu.CompilerParams(
            dimension_semantics=("arbitrary",), vmem_limit_bytes=VMEM_LIMIT),
        name="ffn",
    )(h1, h1, wup, cw, cb, wdn, g, b)


def _prep_proj_weights(w_in, ln_v_g, ln_v_b, w_spatial, b_spatial, ln_kidx_g, ln_kidx_b, w_branch_a):
    o = 0
    segs = []
    for n in (D_A, D_A, D_MODEL, D_MODEL, D_MODEL, IDX_HEADS * IDX_DIM, IDX_DIM, IDX_HEADS,
              D_MODEL, D_MODEL):
        segs.append(w_in[:, o:o + n])
        o += n
    wu, wv, wq, wk, wvv, wqi, wki, wwi, wga, wgb = segs
    kin = w_in.shape[0]
    wqi_pad = jnp.pad(wqi.reshape(kin, IDX_HEADS, IDX_DIM),
                      ((0, 0), (0, 0), (0, IDX_PAD - IDX_DIM))).reshape(kin, IDX_HEADS * IDX_PAD)
    wki_pad = jnp.pad(wki, ((0, 0), (0, IDX_PAD - IDX_DIM)))
    wi = jnp.concatenate([wqi_pad, wki_pad], axis=1)
    wwt = jnp.pad(wwi.T, ((0, 16 - IDX_HEADS), (0, 0)))
    causal = jnp.tril(jnp.ones((CHUNK, CHUNK), dtype=bool))
    ws = jnp.where(causal[None], w_spatial, 0)
    bs = jnp.broadcast_to(b_spatial[:, :, None], (G_A, CHUNK, D_A // G_A))
    pad_k = lambda a: jnp.pad(a, (0, IDX_PAD - IDX_DIM)).reshape(1, IDX_PAD)
    bf = lambda a: a.astype(BF16)
    return (bf(wu), bf(wv), bf(wq), bf(wk), bf(wvv.T), bf(wi), bf(wwt), bf(wga), bf(wgb),
            bf(w_branch_a), bf(ws), bs.astype(F32),
            ln_v_g.reshape(1, -1), ln_v_b.reshape(1, -1), pad_k(ln_kidx_g), pad_k(ln_kidx_b))


def _layer(x, w_in, ln_v_g, ln_v_b, w_spatial, b_spatial, ln_kidx_g, ln_kidx_b, w_branch_a,
           w_branch_b, w_o, ln1_g, ln1_b, w_up, conv_w, conv_b, w_down, ln2_g, ln2_b):
    B, L, D = x.shape
    M = B * L
    topk = min(TOPK_MAX, L // 4)
    x2 = x.reshape(M, D)
    pw = _prep_proj_weights(w_in, ln_v_g, ln_v_b, w_spatial, b_spatial, ln_kidx_g, ln_kidx_b,
                            w_branch_a)
    mixa, q, k, v_t, q_idx, k_idx, w_idx_t, sgb = _run_proj(x2, pw, B, L)
    bias = _run_index(q_idx.reshape(B, L, -1), k_idx.reshape(B, L, -1), w_idx_t, B, L, topk)
    b_out = _run_attn(q.reshape(B, L, D), k.reshape(B, L, D), v_t, bias, B, L)
    h1 = _run_post(mixa, sgb, b_out.reshape(M, D), x2, w_branch_b.astype(BF16), w_o.astype(BF16),
                   ln1_g.reshape(1, -1), ln1_b.reshape(1, -1))
    h2 = _run_ffn(h1, w_up.astype(BF16), conv_w, conv_b.reshape(1, -1), w_down.astype(BF16),
                  ln2_g.reshape(1, -1), ln2_b.reshape(1, -1), L)
    return h2.reshape(B, L, D)


def kernel(x, w_in, ln_v_g, ln_v_b, w_spatial, b_spatial, ln_kidx_g, ln_kidx_b, w_branch_a,
           w_branch_b, w_o, ln1_g, ln1_b, w_up, conv_w, conv_b, w_down, ln2_g, ln2_b):
    h = x
    for i in range(w_in.shape[0]):
        h = _layer(h, w_in[i], ln_v_g[i], ln_v_b[i], w_spatial[i], b_spatial[i], ln_kidx_g[i],
                   ln_kidx_b[i], w_branch_a[i], w_branch_b[i], w_o[i], ln1_g[i], ln1_b[i],
                   w_up[i], conv_w[i], conv_b[i], w_down[i], ln2_g[i], ln2_b[i])
    return h
```

```python
import functools
import math

import jax
import jax.numpy as jnp
from jax import lax
from jax.experimental import pallas as pl
from jax.experimental.pallas import tpu as pltpu

F32 = jnp.float32
BF16 = jnp.bfloat16
I32 = jnp.int32
I16 = jnp.int16

D_MODEL = 1024
D_A = 1024
G_A = 8
CHUNK = 128
N_HEADS = 8
HEAD_DIM = 128
IDX_HEADS = 8
IDX_DIM = 64
IDX_PAD = 128
TOPK_MAX = 256
D_FF = 2816
CONV_W = 3
ALPHA = 2.0 ** 0.25
LN_EPS = 1e-5

INT_MIN = -(2 ** 31)
NEG_BIG = -1e30

PROJ_TM = 256
IDX_TQ = 256
IDX_SC = 512
ATT_TQ = 256
ATT_TK = 512
POST_TM = 512
FFN_TM = 512
FFN_TF = 256
HALO = 16

VMEM_LIMIT = 56 * 1024 * 1024

NT_DIMS = (((1,), (1,)), ((), ()))


def _dot(a, b):
    return jnp.dot(a, b, preferred_element_type=F32)


def _dot_nt(a, b):
    return lax.dot_general(a, b, NT_DIMS, preferred_element_type=F32)


def _layer_norm(x, g, b):
    mu = jnp.mean(x, axis=-1, keepdims=True)
    d = x - mu
    var = jnp.mean(d * d, axis=-1, keepdims=True)
    return d * lax.rsqrt(var + LN_EPS) * g + b


def _const_spec(shape):
    nd = len(shape)
    return pl.BlockSpec(shape, lambda *_: (0,) * nd, pipeline_mode=pl.Buffered(1))


def _proj_kernel(x_ref, wu_ref, wv_ref, wq_ref, wk_ref, wvt_ref, wi_ref, wwt_ref, wga_ref,
                 wgb_ref, wba_ref, ws_ref, bs_ref, lnvg_ref, lnvb_ref, lnkg_ref, lnkb_ref,
                 mixa_ref, q_ref, k_ref, vt_ref, qi_ref, ki_ref, wit_ref, sgb_ref, a_scr):
    tm = x_ref.shape[0]
    xb = x_ref[...].astype(BF16)

    u = jax.nn.gelu(_dot(xb, wu_ref[...]))
    v = _layer_norm(jax.nn.gelu(_dot(xb, wv_ref[...])), lnvg_ref[...], lnvb_ref[...]).astype(BF16)
    for c in range(tm // CHUNK):
        rows = slice(c * CHUNK, (c + 1) * CHUNK)
        for g in range(G_A):
            cols = slice(g * CHUNK, (g + 1) * CHUNK)
            mixed = _dot(ws_ref[g], v[rows, cols]) + bs_ref[g]
            a_scr[rows, cols] = (u[rows, cols] * mixed).astype(BF16)
    gate_a = jax.nn.sigmoid(_dot(xb, wga_ref[...]))
    mixa_ref[...] = gate_a * _dot(a_scr[...], wba_ref[...])

    q_ref[...] = _dot(xb, wq_ref[...]).astype(BF16)
    k_ref[...] = _dot(xb, wk_ref[...]).astype(BF16)
    vt_ref[0] = _dot_nt(wvt_ref[...], xb).astype(BF16)

    r = _dot(xb, wi_ref[...])
    qi_ref[...] = r[:, :IDX_HEADS * IDX_PAD].astype(BF16)
    kraw = r[:, IDX_HEADS * IDX_PAD:]
    lane = lax.broadcasted_iota(I32, kraw.shape, 1)
    valid = lane < IDX_DIM
    mu = jnp.sum(kraw, axis=1, keepdims=True) * (1.0 / IDX_DIM)
    d = jnp.where(valid, kraw - mu, 0.0)
    var = jnp.sum(d * d, axis=1, keepdims=True) * (1.0 / IDX_DIM)
    ki_ref[...] = (d * lax.rsqrt(var + LN_EPS) * lnkg_ref[...] + lnkb_ref[...]).astype(BF16)
    wt = _dot_nt(wwt_ref[...], xb)
    wit_ref[0] = wt[:IDX_HEADS, :] * (IDX_HEADS ** -0.5 * IDX_DIM ** -0.5)

    sgb_ref[...] = jax.nn.sigmoid(_dot(xb, wgb_ref[...]))


def _run_proj(x2, w, B, L):
    M = x2.shape[0]
    tm = PROJ_TM
    tiles_per_seq = L // tm
    row = lambda i: (i, 0)
    seq = lambda i: (i // tiles_per_seq, 0, i % tiles_per_seq)
    in_specs = [pl.BlockSpec((tm, D_MODEL), row)] + [_const_spec(a.shape) for a in w]
    out_shape = (
        jax.ShapeDtypeStruct((M, D_MODEL), F32),
        jax.ShapeDtypeStruct((M, D_MODEL), BF16),
        jax.ShapeDtypeStruct((M, D_MODEL), BF16),
        jax.ShapeDtypeStruct((B, D_MODEL, L), BF16),
        jax.ShapeDtypeStruct((M, IDX_HEADS * IDX_PAD), BF16),
        jax.ShapeDtypeStruct((M, IDX_PAD), BF16),
        jax.ShapeDtypeStruct((B, IDX_HEADS, L), F32),
        jax.ShapeDtypeStruct((M, D_MODEL), F32),
    )
    out_specs = (
        pl.BlockSpec((tm, D_MODEL), row),
        pl.BlockSpec((tm, D_MODEL), row),
        pl.BlockSpec((tm, D_MODEL), row),
        pl.BlockSpec((1, D_MODEL, tm), seq),
        pl.BlockSpec((tm, IDX_HEADS * IDX_PAD), row),
        pl.BlockSpec((tm, IDX_PAD), row),
        pl.BlockSpec((1, IDX_HEADS, tm), seq),
        pl.BlockSpec((tm, D_MODEL), row),
    )
    return pl.pallas_call(
        _proj_kernel,
        grid=(M // tm,),
        in_specs=in_specs,
        out_specs=out_specs,
        out_shape=out_shape,
        scratch_shapes=[pltpu.VMEM((tm, D_A), BF16)],
        compiler_params=pltpu.CompilerParams(
            dimension_semantics=("arbitrary",), vmem_limit_bytes=VMEM_LIMIT),
        name="proj",
    )(x2, *w)


def _sum_tiles(m, rows, n_chains):
    n = m.shape[0] // rows
    parts = [m[r * rows:(r + 1) * rows, :] for r in range(n)]
    accs = parts[:n_chains]
    for r in range(n_chains, n):
        accs[r % n_chains] = accs[r % n_chains] + parts[r]
    while len(accs) > 1:
        accs = [accs[a] + accs[a + 1] for a in range(0, len(accs), 2)]
    return accs[0]


def _ordered_key(x):
    bits = lax.bitcast_convert_type(x, I32)
    key = bits ^ ((bits >> 31) & 0x7FFFFFFF)
    return jnp.where(key == -1, 0, key)


def _index_kernel(q_ref, k_ref, w_ref, bias_ref, key_scr, half_scr, *, topk, n_chunks_total):
    tq = q_ref.shape[1]
    sc = IDX_SC
    i = pl.program_id(1)
    n_chunks = ((i + 1) * tq + sc - 1) // sc
    last = n_chunks - 1
    t_idx = i * tq + lax.broadcasted_iota(I32, (1, tq), 1)
    row_iota = lax.broadcasted_iota(I32, (sc, tq), 0)
    int_max = 2 ** 31 - 1
    i16_bias = 2 ** 15

    def chunk_start(j):
        return pl.multiple_of(j * sc, sc)

    def score_chunk(j, on_diagonal):
        kc = k_ref[0, pl.ds(chunk_start(j), sc), :]
        acc = jnp.zeros((sc, tq), F32)
        for h in range(IDX_HEADS):
            qh = q_ref[0, :, h * IDX_PAD:(h + 1) * IDX_PAD]
            acc = acc + w_ref[0, h:h + 1, :] * jnp.maximum(_dot_nt(kc, qh), 0.0)
        key = _ordered_key(acc)
        if on_diagonal:
            key = jnp.where(row_iota + j * sc <= t_idx, key, INT_MIN)
        key_scr[pl.ds(chunk_start(j), sc), :] = key
        half_scr[pl.ds(chunk_start(j), sc), :] = (key >> 16).astype(I16)

    def score_body(j, carry):
        score_chunk(j, False)
        return carry

    lax.fori_loop(0, last, score_body, 0)
    score_chunk(last, True)

    def count(pred):
        def body(j, acc):
            blk = key_scr[pl.ds(chunk_start(j), sc), :]
            return acc + _sum_tiles(jnp.where(pred(blk, j), 1, 0), 8, 8)
        acc = lax.fori_loop(0, n_chunks, body, jnp.zeros((8, tq), I32))
        return jnp.sum(acc, axis=0, keepdims=True)

    def count_half(pred):
        def body(j, acc):
            blk = half_scr[pl.ds(chunk_start(j), sc), :]
            ones = jnp.where(pred(blk), jnp.int16(1), jnp.int16(0))
            return acc + _sum_tiles(ones, 16, 4).astype(I32)
        acc = lax.fori_loop(0, n_chunks, body, jnp.zeros((16, tq), I32))
        return jnp.sum(acc, axis=0, keepdims=True)

    def search_half(need):
        def bit_body(b, cand):
            trial = cand | jnp.left_shift(jnp.int32(1), 15 - b)
            trial16 = (trial - i16_bias).astype(I16)
            c = count_half(lambda blk: blk >= trial16)
            return jnp.where(c >= need, trial, cand)
        return lax.fori_loop(0, 16, bit_body, jnp.zeros((1, tq), I32))

    hi = search_half(topk) - i16_bias
    hi16 = hi.astype(I16)
    n_above = count_half(lambda blk: blk > hi16)

    def low_body(j, carry):
        key = key_scr[pl.ds(chunk_start(j), sc), :]
        low = jnp.where((key >> 16) == hi, (key & 0xFFFF) - i16_bias, -i16_bias)
        half_scr[pl.ds(chunk_start(j), sc), :] = low.astype(I16)
        return carry

    lax.fori_loop(0, n_chunks, low_body, 0)
    lo = search_half(topk - n_above)
    thr = jnp.left_shift(hi, 16) | lo

    lo16 = (lo - i16_bias).astype(I16)
    n_gt = n_above + count_half(lambda blk: blk > lo16)
    n_ge = count(lambda blk, j: blk >= thr)
    n_take = topk - n_gt
    need_cut = jnp.max(jnp.where(n_ge - n_gt > n_take, 1, 0)) > 0

    def write_mask(j, sel):
        bias_ref[0, pl.ds(chunk_start(j), sc), :] = jnp.where(sel, 0.0, NEG_BIG).astype(BF16)

    @pl.when(jnp.logical_not(need_cut))
    def _():
        def mask_body(j, carry):
            write_mask(j, key_scr[pl.ds(chunk_start(j), sc), :] >= thr)
            return carry

        lax.fori_loop(0, last, mask_body, 0)
        blk = key_scr[pl.ds(chunk_start(last), sc), :]
        write_mask(last, (blk >= thr) & (row_iota + last * sc <= t_idx))

    @pl.when(need_cut)
    def _():
        nbits = max(1, (n_chunks_total * sc - 1).bit_length())

        def tie_body(b, lo_idx):
            cand = lo_idx + jnp.left_shift(jnp.int32(1), nbits - 1 - b)
            f = count(lambda blk, j: jnp.where(blk == thr, row_iota + j * sc, int_max) <= cand)
            return jnp.where(f < n_take, cand, lo_idx)

        cut = lax.fori_loop(0, nbits, tie_body, jnp.full((1, tq), -1, I32)) + 1

        def mask_body(j, carry):
            blk = key_scr[pl.ds(chunk_start(j), sc), :]
            s_idx = row_iota + j * sc
            sel = ((blk > thr) | ((blk == thr) & (s_idx <= cut))) & (s_idx <= t_idx)
            write_mask(j, sel)
            return carry

        lax.fori_loop(0, n_chunks, mask_body, 0)

    def fill_body(j, carry):
        bias_ref[0, pl.ds(chunk_start(j), sc), :] = jnp.full((sc, tq), NEG_BIG, BF16)
        return carry

    lax.fori_loop(n_chunks, n_chunks_total, fill_body, 0)


def _run_index(q_idx, k_idx, w_idx_t, B, L, topk):
    tq = min(IDX_TQ, L)
    kernel = functools.partial(_index_kernel, topk=topk, n_chunks_total=L // IDX_SC)
    return pl.pallas_call(
        kernel,
        grid=(B, L // tq),
        in_specs=[
            pl.BlockSpec((1, tq, IDX_HEADS * IDX_PAD), lambda b, i: (b, i, 0)),
            pl.BlockSpec((1, L, IDX_PAD), lambda b, i: (b, 0, 0)),
            pl.BlockSpec((1, IDX_HEADS, tq), lambda b, i: (b, 0, i)),
        ],
        out_specs=pl.BlockSpec((1, L, tq), lambda b, i: (b, 0, i)),
        out_shape=jax.ShapeDtypeStruct((B, L, L), BF16),
        scratch_shapes=[pltpu.VMEM((L, tq), I32), pltpu.VMEM((L, tq), I16)],
        compiler_params=pltpu.CompilerParams(
            dimension_semantics=("arbitrary", "arbitrary"), vmem_limit_bytes=VMEM_LIMIT),
        name="indexer",
    )(q_idx, k_idx, w_idx_t)


def _attn_kernel(qblk_ref, kblk_ref, q_ref, k_ref, vt_ref, bias_ref, o_ref, m_scr, l_scr, acc_scr):
    tq = q_ref.shape[1]
    tk = k_ref.shape[1]
    step = pl.program_id(1)
    i = qblk_ref[step]
    j = kblk_ref[step]
    n_kblocks = ((i + 1) * tq + tk - 1) // tk
    log2_scale = math.log2(math.e) / math.sqrt(HEAD_DIM)

    @pl.when(j == 0)
    def _():
        m_scr[...] = jnp.full(m_scr.shape, NEG_BIG, F32)
        l_scr[...] = jnp.zeros(l_scr.shape, F32)
        acc_scr[...] = jnp.zeros(acc_scr.shape, F32)

    bias = bias_ref[0].astype(F32)

    def logits(h):
        hs = slice(h * HEAD_DIM, (h + 1) * HEAD_DIM)
        return _dot_nt(k_ref[0, :, hs], q_ref[0, :, hs])

    raw_next = logits(0)
    for h in range(N_HEADS):
        hs = slice(h * HEAD_DIM, (h + 1) * HEAD_DIM)
        raw = raw_next
        if h + 1 < N_HEADS:
            raw_next = logits(h + 1)
        s = raw * log2_scale + bias
        m_prev = m_scr[h]
        m_new = jnp.maximum(m_prev, jnp.max(s, axis=0, keepdims=True))
        alpha = jnp.exp2(m_prev - m_new)
        p = jnp.exp2(s - m_new[0:1, :])
        l_scr[h] = alpha * l_scr[h] + jnp.sum(p, axis=0, keepdims=True)
        pv = _dot(vt_ref[0, hs, :], p.astype(BF16))
        acc_scr[hs, :] = alpha[0:1, :] * acc_scr[hs, :] + pv
        m_scr[h] = m_new

    @pl.when(j == n_kblocks - 1)
    def _():
        for h in range(N_HEADS):
            hs = slice(h * HEAD_DIM, (h + 1) * HEAD_DIM)
            out_t = acc_scr[hs, :] / l_scr[h][0:1, :]
            o_ref[0, :, hs] = out_t.T.astype(BF16)


def _run_attn(q, k, v_t, bias, B, L):
    tq = min(ATT_TQ, L)
    tk = min(ATT_TK, L)

    pairs = [(i, j) for i in range(L // tq) for j in range(((i + 1) * tq + tk - 1) // tk)]
    qblk = jnp.asarray([p[0] for p in pairs], I32)
    kblk = jnp.asarray([p[1] for p in pairs], I32)

    grid_spec = pltpu.PrefetchScalarGridSpec(
        num_scalar_prefetch=2,
        grid=(B, len(pairs)),
        in_specs=[
            pl.BlockSpec((1, tq, D_MODEL), lambda b, s, qb, kb: (b, qb[s], 0)),
            pl.BlockSpec((1, tk, D_MODEL), lambda b, s, qb, kb: (b, kb[s], 0)),
            pl.BlockSpec((1, D_MODEL, tk), lambda b, s, qb, kb: (b, 0, kb[s])),
            pl.BlockSpec((1, tk, tq), lambda b, s, qb, kb: (b, kb[s], qb[s])),
        ],
        out_specs=pl.BlockSpec((1, tq, D_MODEL), lambda b, s, qb, kb: (b, qb[s], 0)),
        scratch_shapes=[
            pltpu.VMEM((N_HEADS, 8, tq), F32),
            pltpu.VMEM((N_HEADS, 8, tq), F32),
            pltpu.VMEM((D_MODEL, tq), F32),
        ],
    )
    return pl.pallas_call(
        _attn_kernel,
        grid_spec=grid_spec,
        out_shape=jax.ShapeDtypeStruct((B, L, D_MODEL), BF16),
        compiler_params=pltpu.CompilerParams(
            dimension_semantics=("arbitrary", "arbitrary"), vmem_limit_bytes=VMEM_LIMIT),
        name="attention",
    )(qblk, kblk, q, k, v_t, bias)


def _post_kernel(mixa_ref, sgb_ref, bo_ref, x_ref, wbb_ref, wo_ref, g_ref, b_ref, h_ref):
    mix = mixa_ref[...] + sgb_ref[...] * _dot(bo_ref[...], wbb_ref[...])
    y = _dot(mix.astype(BF16), wo_ref[...])
    h_ref[...] = _layer_norm(ALPHA * x_ref[...] + y, g_ref[...], b_ref[...])


def _run_post(mixa, sgb, b_out, x2, wbb, wo, g, b):
    M = x2.shape[0]
    tm = POST_TM
    row = lambda i: (i, 0)
    tile = pl.BlockSpec((tm, D_MODEL), row)
    return pl.pallas_call(
        _post_kernel,
        grid=(M // tm,),
        in_specs=[tile, tile, tile, tile, _const_spec(wbb.shape), _const_spec(wo.shape),
                  _const_spec(g.shape), _const_spec(b.shape)],
        out_specs=tile,
        out_shape=jax.ShapeDtypeStruct((M, D_MODEL), F32),
        compiler_params=pltpu.CompilerParams(
            dimension_semantics=("arbitrary",), vmem_limit_bytes=VMEM_LIMIT),
        name="post",
    )(mixa, sgb, b_out, x2, wbb, wo, g, b)


def _ffn_kernel(h_ref, halo_ref, wup_ref, cw_ref, cb_ref, wdn_ref, g_ref, b_ref, o_ref,
                *, tiles_per_seq):
    tm = h_ref.shape[0]
    i = pl.program_id(0)
    h = h_ref[...]
    keep = (i % tiles_per_seq != 0).astype(F32)
    xe = jnp.concatenate([halo_ref[...] * keep, h], axis=0).astype(BF16)

    def conv(up, col0):
        cols = slice(col0, col0 + FFN_TF)
        w = cw_ref[:, cols]
        y = cb_ref[:, cols] + w[2:3, :] * up[HALO:, :]
        y = y + w[1:2, :] * up[HALO - 1:HALO - 1 + tm, :]
        return y + w[0:1, :] * up[HALO - 2:HALO - 2 + tm, :]

    acc = jnp.zeros((tm, D_MODEL), F32)
    for c in range(D_FF // FFN_TF):
        g0 = c * FFN_TF
        v0 = D_FF + c * FFN_TF
        gate = conv(_dot(xe, wup_ref[:, g0:g0 + FFN_TF]), g0)
        val = conv(_dot(xe, wup_ref[:, v0:v0 + FFN_TF]), v0)
        act = (gate * jax.nn.sigmoid(gate) * val).astype(BF16)
        acc = acc + _dot(act, wdn_ref[g0:g0 + FFN_TF, :])
    o_ref[...] = _layer_norm(ALPHA * h + acc, g_ref[...], b_ref[...])


def _run_ffn(h1, wup, cw, cb, wdn, g, b, L):
    M = h1.shape[0]
    tm = min(FFN_TM, L)
    kernel = functools.partial(_ffn_kernel, tiles_per_seq=L // tm)
    halo_blocks = tm // HALO
    return pl.pallas_call(
        kernel,
        grid=(M // tm,),
        in_specs=[
            pl.BlockSpec((tm, D_MODEL), lambda i: (i, 0)),
            pl.BlockSpec((HALO, D_MODEL), lambda i: (jnp.maximum(i * halo_blocks - 1, 0), 0)),
            _const_spec(wup.shape), _const_spec(cw.shape), _const_spec(cb.shape),
            _const_spec(wdn.shape), _const_spec(g.shape), _const_spec(b.shape),
        ],
        out_specs=pl.BlockSpec((tm, D_MODEL), lambda i: (i, 0)),
        out_shape=jax.ShapeDtypeStruct((M, D_MODEL), F32),
        compiler_params=pltpu.CompilerParams(
            dimension_semantics=("arbitrary",), vmem_limit_bytes=VMEM_LIMIT),
        name="ffn",
    )(h1, h1, wup, cw, cb, wdn, g, b)


def _prep_proj_weights(w_in, ln_v_g, ln_v_b, w_spatial, b_spatial, ln_kidx_g, ln_kidx_b, w_branch_a):
    o = 0
    segs = []
    for n in (D_A, D_A, D_MODEL, D_MODEL, D_MODEL, IDX_HEADS * IDX_DIM, IDX_DIM, IDX_HEADS,
              D_MODEL, D_MODEL):
        segs.append(w_in[:, o:o + n])
        o += n
    wu, wv, wq, wk, wvv, wqi, wki, wwi, wga, wgb = segs
    kin = w_in.shape[0]
    wqi_pad = jnp.pad(wqi.reshape(kin, IDX_HEADS, IDX_DIM),
                      ((0, 0), (0, 0), (0, IDX_PAD - IDX_DIM))).reshape(kin, IDX_HEADS * IDX_PAD)
    wki_pad = jnp.pad(wki, ((0, 0), (0, IDX_PAD - IDX_DIM)))
    wi = jnp.concatenate([wqi_pad, wki_pad], axis=1)
    wwt = jnp.pad(wwi.T, ((0, 16 - IDX_HEADS), (0, 0)))
    causal = jnp.tril(jnp.ones((CHUNK, CHUNK), dtype=bool))
    ws = jnp.where(causal[None], w_spatial, 0)
    bs = jnp.broadcast_to(b_spatial[:, :, None], (G_A, CHUNK, D_A // G_A))
    pad_k = lambda a: jnp.pad(a, (0, IDX_PAD - IDX_DIM)).reshape(1, IDX_PAD)
    bf = lambda a: a.astype(BF16)
    return (bf(wu), bf(wv), bf(wq), bf(wk), bf(wvv.T), bf(wi), bf(wwt), bf(wga), bf(wgb),
            bf(w_branch_a), bf(ws), bs.astype(F32),
            ln_v_g.reshape(1, -1), ln_v_b.reshape(1, -1), pad_k(ln_kidx_g), pad_k(ln_kidx_b))


def _layer(x, w_in, ln_v_g, ln_v_b, w_spatial, b_spatial, ln_kidx_g, ln_kidx_b, w_branch_a,
           w_branch_b, w_o, ln1_g, ln1_b, w_up, conv_w, conv_b, w_down, ln2_g, ln2_b):
    B, L, D = x.shape
    M = B * L
    topk = min(TOPK_MAX, L // 4)
    x2 = x.reshape(M, D)
    pw = _prep_proj_weights(w_in, ln_v_g, ln_v_b, w_spatial, b_spatial, ln_kidx_g, ln_kidx_b,
                            w_branch_a)
    mixa, q, k, v_t, q_idx, k_idx, w_idx_t, sgb = _run_proj(x2, pw, B, L)
    bias = _run_index(q_idx.reshape(B, L, -1), k_idx.reshape(B, L, -1), w_idx_t, B, L, topk)
    b_out = _run_attn(q.reshape(B, L, D), k.reshape(B, L, D), v_t, bias, B, L)
    h1 = _run_post(mixa, sgb, b_out.reshape(M, D), x2, w_branch_b.astype(BF16), w_o.astype(BF16),
                   ln1_g.reshape(1, -1), ln1_b.reshape(1, -1))
    h2 = _run_ffn(h1, w_up.astype(BF16), conv_w, conv_b.reshape(1, -1), w_down.astype(BF16),
                  ln2_g.reshape(1, -1), ln2_b.reshape(1, -1), L)
    return h2.reshape(B, L, D)


def kernel(x, w_in, ln_v_g, ln_v_b, w_spatial, b_spatial, ln_kidx_g, ln_kidx_b, w_branch_a,
           w_branch_b, w_o, ln1_g, ln1_b, w_up, conv_w, conv_b, w_down, ln2_g, ln2_b):
    h = x
    for i in range(w_in.shape[0]):
        h = _layer(h, w_in[i], ln_v_g[i], ln_v_b[i], w_spatial[i], b_spatial[i], ln_kidx_g[i],
                   ln_kidx_b[i], w_branch_a[i], w_branch_b[i], w_o[i], ln1_g[i], ln1_b[i],
                   w_up[i], conv_w[i], conv_b[i], w_down[i], ln2_g[i], ln2_b[i])
    return h
```

```python
import functools
import math

import jax
import jax.numpy as jnp
from jax import lax
from jax.experimental import pallas as pl
from jax.experimental.pallas import tpu as pltpu

F32 = jnp.float32
BF16 = jnp.bfloat16
I32 = jnp.int32
I16 = jnp.int16

D_MODEL = 1024
D_A = 1024
G_A = 8
CHUNK = 128
N_HEADS = 8
HEAD_DIM = 128
IDX_HEADS = 8
IDX_DIM = 64
IDX_PAD = 128
TOPK_MAX = 256
D_FF = 2816
CONV_W = 3
ALPHA = 2.0 ** 0.25
LN_EPS = 1e-5

INT_MIN = -(2 ** 31)
NEG_BIG = -1e30

PROJ_TM = 256
IDX_TQ = 256
IDX_SC = 512
ATT_TQ = 256
ATT_TK = 512
ATT_LOOKAHEAD = 2
POST_TM = 512
POST_SUB = 256
FFN_TM = 512
FFN_TF = 256
HALO = 16

VMEM_LIMIT = 56 * 1024 * 1024

NT_DIMS = (((1,), (1,)), ((), ()))


def _dot(a, b):
    return jnp.dot(a, b, preferred_element_type=F32)


def _dot_nt(a, b):
    return lax.dot_general(a, b, NT_DIMS, preferred_element_type=F32)


def _layer_norm(x, g, b):
    mu = jnp.mean(x, axis=-1, keepdims=True)
    d = x - mu
    var = jnp.mean(d * d, axis=-1, keepdims=True)
    return d * lax.rsqrt(var + LN_EPS) * g + b


def _const_spec(shape):
    nd = len(shape)
    return pl.BlockSpec(shape, lambda *_: (0,) * nd, pipeline_mode=pl.Buffered(1))


def _proj_kernel(x_ref, wu_ref, wv_ref, wq_ref, wk_ref, wvt_ref, wi_ref, wwt_ref, wga_ref,
                 wgb_ref, wba_ref, ws_ref, bs_ref, lnvg_ref, lnvb_ref, lnkg_ref, lnkb_ref,
                 mixa_ref, q_ref, k_ref, vt_ref, qi_ref, ki_ref, wit_ref, sgb_ref, a_scr):
    tm = x_ref.shape[0]
    xb = x_ref[...].astype(BF16)

    raw_u = _dot(xb, wu_ref[...])
    raw_v = _dot(xb, wv_ref[...])
    raw_q = _dot(xb, wq_ref[...])

    u = jax.nn.gelu(raw_u)
    v = _layer_norm(jax.nn.gelu(raw_v), lnvg_ref[...], lnvb_ref[...]).astype(BF16)
    raw_k = _dot(xb, wk_ref[...])
    q_ref[...] = raw_q.astype(BF16)
    for c in range(tm // CHUNK):
        rows = slice(c * CHUNK, (c + 1) * CHUNK)
        for g in range(G_A):
            cols = slice(g * CHUNK, (g + 1) * CHUNK)
            mixed = _dot(ws_ref[g], v[rows, cols]) + bs_ref[g]
            a_scr[rows, cols] = (u[rows, cols] * mixed).astype(BF16)
    raw_ga = _dot(xb, wga_ref[...])
    k_ref[...] = raw_k.astype(BF16)
    raw_vt = _dot_nt(wvt_ref[...], xb)
    raw_mix = _dot(a_scr[...], wba_ref[...])
    gate_a = jax.nn.sigmoid(raw_ga)
    vt_ref[0] = raw_vt.astype(BF16)
    r = _dot(xb, wi_ref[...])
    mixa_ref[...] = gate_a * raw_mix
    raw_gb = _dot(xb, wgb_ref[...])

    qi_ref[...] = r[:, :IDX_HEADS * IDX_PAD].astype(BF16)
    kraw = r[:, IDX_HEADS * IDX_PAD:]
    lane = lax.broadcasted_iota(I32, kraw.shape, 1)
    valid = lane < IDX_DIM
    mu = jnp.sum(kraw, axis=1, keepdims=True) * (1.0 / IDX_DIM)
    d = jnp.where(valid, kraw - mu, 0.0)
    var = jnp.sum(d * d, axis=1, keepdims=True) * (1.0 / IDX_DIM)
    ki_ref[...] = (d * lax.rsqrt(var + LN_EPS) * lnkg_ref[...] + lnkb_ref[...]).astype(BF16)
    wt = _dot_nt(wwt_ref[...], xb)
    sgb_ref[...] = jax.nn.sigmoid(raw_gb)
    wit_ref[0] = wt[:IDX_HEADS, :] * (IDX_HEADS ** -0.5 * IDX_DIM ** -0.5)


def _run_proj(x2, w, B, L):
    M = x2.shape[0]
    tm = PROJ_TM
    tiles_per_seq = L // tm
    row = lambda i: (i, 0)
    seq = lambda i: (i // tiles_per_seq, 0, i % tiles_per_seq)
    in_specs = [pl.BlockSpec((tm, D_MODEL), row)] + [_const_spec(a.shape) for a in w]
    out_shape = (
        jax.ShapeDtypeStruct((M, D_MODEL), F32),
        jax.ShapeDtypeStruct((M, D_MODEL), BF16),
        jax.ShapeDtypeStruct((M, D_MODEL), BF16),
        jax.ShapeDtypeStruct((B, D_MODEL, L), BF16),
        jax.ShapeDtypeStruct((M, IDX_HEADS * IDX_PAD), BF16),
        jax.ShapeDtypeStruct((M, IDX_PAD), BF16),
        jax.ShapeDtypeStruct((B, IDX_HEADS, L), F32),
        jax.ShapeDtypeStruct((M, D_MODEL), F32),
    )
    out_specs = (
        pl.BlockSpec((tm, D_MODEL), row),
        pl.BlockSpec((tm, D_MODEL), row),
        pl.BlockSpec((tm, D_MODEL), row),
        pl.BlockSpec((1, D_MODEL, tm), seq),
        pl.BlockSpec((tm, IDX_HEADS * IDX_PAD), row),
        pl.BlockSpec((tm, IDX_PAD), row),
        pl.BlockSpec((1, IDX_HEADS, tm), seq),
        pl.BlockSpec((tm, D_MODEL), row),
    )
    return pl.pallas_call(
        _proj_kernel,
        grid=(M // tm,),
        in_specs=in_specs,
        out_specs=out_specs,
        out_shape=out_shape,
        scratch_shapes=[pltpu.VMEM((tm, D_A), BF16)],
        compiler_params=pltpu.CompilerParams(
            dimension_semantics=("arbitrary",), vmem_limit_bytes=VMEM_LIMIT),
        name="proj",
    )(x2, *w)


def _sum_tiles(m, rows, n_chains):
    n = m.shape[0] // rows
    parts = [m[r * rows:(r + 1) * rows, :] for r in range(n)]
    accs = parts[:n_chains]
    for r in range(n_chains, n):
        accs[r % n_chains] = accs[r % n_chains] + parts[r]
    while len(accs) > 1:
        accs = [accs[a] + accs[a + 1] for a in range(0, len(accs), 2)]
    return accs[0]


def _ordered_key(x):
    bits = lax.bitcast_convert_type(x, I32)
    key = bits ^ ((bits >> 31) & 0x7FFFFFFF)
    return jnp.where(key == -1, 0, key)


def _index_kernel(q_ref, k_ref, w_ref, bias_ref, key_scr, half_scr, *, topk, n_chunks_total):
    tq = q_ref.shape[1]
    sc = IDX_SC
    i = pl.program_id(1)
    n_chunks = ((i + 1) * tq + sc - 1) // sc
    last = n_chunks - 1
    t_idx = i * tq + lax.broadcasted_iota(I32, (1, tq), 1)
    row_iota = lax.broadcasted_iota(I32, (sc, tq), 0)
    int_max = 2 ** 31 - 1
    i16_bias = 2 ** 15

    def chunk_start(j):
        return pl.multiple_of(j * sc, sc)

    def score_chunk(j, on_diagonal):
        kc = k_ref[0, pl.ds(chunk_start(j), sc), :]
        acc = jnp.zeros((sc, tq), F32)
        for h in range(IDX_HEADS):
            qh = q_ref[0, :, h * IDX_PAD:(h + 1) * IDX_PAD]
            acc = acc + w_ref[0, h:h + 1, :] * jnp.maximum(_dot_nt(kc, qh), 0.0)
        key = _ordered_key(acc)
        if on_diagonal:
            key = jnp.where(row_iota + j * sc <= t_idx, key, INT_MIN)
        key_scr[pl.ds(chunk_start(j), sc), :] = key
        half_scr[pl.ds(chunk_start(j), sc), :] = (key >> 16).astype(I16)

    def score_body(j, carry):
        score_chunk(j, False)
        return carry

    lax.fori_loop(0, last, score_body, 0)
    score_chunk(last, True)

    def count(pred):
        def body(j, acc):
            blk = key_scr[pl.ds(chunk_start(j), sc), :]
            return acc + _sum_tiles(jnp.where(pred(blk, j), 1, 0), 8, 8)
        acc = lax.fori_loop(0, n_chunks, body, jnp.zeros((8, tq), I32))
        return jnp.sum(acc, axis=0, keepdims=True)

    def count_half(pred):
        def body(j, acc):
            blk = half_scr[pl.ds(chunk_start(j), sc), :]
            ones = jnp.where(pred(blk), jnp.int16(1), jnp.int16(0))
            return acc + _sum_tiles(ones, 16, 4).astype(I32)
        acc = lax.fori_loop(0, n_chunks, body, jnp.zeros((16, tq), I32))
        return jnp.sum(acc, axis=0, keepdims=True)

    def search_half(need):
        def bit_body(b, cand):
            trial = cand | jnp.left_shift(jnp.int32(1), 15 - b)
            trial16 = (trial - i16_bias).astype(I16)
            c = count_half(lambda blk: blk >= trial16)
            return jnp.where(c >= need, trial, cand)
        return lax.fori_loop(0, 16, bit_body, jnp.zeros((1, tq), I32))

    hi = search_half(topk) - i16_bias
    hi16 = hi.astype(I16)
    n_above = count_half(lambda blk: blk > hi16)

    def low_body(j, carry):
        key = key_scr[pl.ds(chunk_start(j), sc), :]
        low = jnp.where((key >> 16) == hi, (key & 0xFFFF) - i16_bias, -i16_bias)
        half_scr[pl.ds(chunk_start(j), sc), :] = low.astype(I16)
        return carry

    lax.fori_loop(0, n_chunks, low_body, 0)
    lo = search_half(topk - n_above)
    thr = jnp.left_shift(hi, 16) | lo

    lo16 = (lo - i16_bias).astype(I16)
    n_gt = n_above + count_half(lambda blk: blk > lo16)
    n_ge = count(lambda blk, j: blk >= thr)
    n_take = topk - n_gt
    need_cut = jnp.max(jnp.where(n_ge - n_gt > n_take, 1, 0)) > 0

    def write_mask(j, sel):
        bias_ref[0, pl.ds(chunk_start(j), sc), :] = jnp.where(sel, 0.0, NEG_BIG).astype(BF16)

    @pl.when(jnp.logical_not(need_cut))
    def _():
        def mask_body(j, carry):
            write_mask(j, key_scr[pl.ds(chunk_start(j), sc), :] >= thr)
            return carry

        lax.fori_loop(0, last, mask_body, 0)
        blk = key_scr[pl.ds(chunk_start(last), sc), :]
        write_mask(last, (blk >= thr) & (row_iota + last * sc <= t_idx))

    @pl.when(need_cut)
    def _():
        nbits = max(1, (n_chunks_total * sc - 1).bit_length())

        def tie_body(b, lo_idx):
            cand = lo_idx + jnp.left_shift(jnp.int32(1), nbits - 1 - b)
            f = count(lambda blk, j: jnp.where(blk == thr, row_iota + j * sc, int_max) <= cand)
            return jnp.where(f < n_take, cand, lo_idx)

        cut = lax.fori_loop(0, nbits, tie_body, jnp.full((1, tq), -1, I32)) + 1

        def mask_body(j, carry):
            blk = key_scr[pl.ds(chunk_start(j), sc), :]
            s_idx = row_iota + j * sc
            sel = ((blk > thr) | ((blk == thr) & (s_idx <= cut))) & (s_idx <= t_idx)
            write_mask(j, sel)
            return carry

        lax.fori_loop(0, n_chunks, mask_body, 0)

    def fill_body(j, carry):
        bias_ref[0, pl.ds(chunk_start(j), sc), :] = jnp.full((sc, tq), NEG_BIG, BF16)
        return carry

    lax.fori_loop(n_chunks, n_chunks_total, fill_body, 0)


def _run_index(q_idx, k_idx, w_idx_t, B, L, topk):
    tq = min(IDX_TQ, L)
    kernel = functools.partial(_index_kernel, topk=topk, n_chunks_total=L // IDX_SC)
    return pl.pallas_call(
        kernel,
        grid=(B, L // tq),
        in_specs=[
            pl.BlockSpec((1, tq, IDX_HEADS * IDX_PAD), lambda b, i: (b, i, 0)),
            pl.BlockSpec((1, L, IDX_PAD), lambda b, i: (b, 0, 0)),
            pl.BlockSpec((1, IDX_HEADS, tq), lambda b, i: (b, 0, i)),
        ],
        out_specs=pl.BlockSpec((1, L, tq), lambda b, i: (b, 0, i)),
        out_shape=jax.ShapeDtypeStruct((B, L, L), BF16),
        scratch_shapes=[pltpu.VMEM((L, tq), I32), pltpu.VMEM((L, tq), I16)],
        compiler_params=pltpu.CompilerParams(
            dimension_semantics=("arbitrary", "arbitrary"), vmem_limit_bytes=VMEM_LIMIT),
        name="indexer",
    )(q_idx, k_idx, w_idx_t)


def _attn_kernel(qblk_ref, kblk_ref, q_ref, k_ref, vt_ref, bias_ref, o_ref, m_scr, l_scr, acc_scr):
    tq = q_ref.shape[1]
    tk = k_ref.shape[1]
    step = pl.program_id(1)
    i = qblk_ref[step]
    j = kblk_ref[step]
    n_kblocks = ((i + 1) * tq + tk - 1) // tk
    log2_scale = math.log2(math.e) / math.sqrt(HEAD_DIM)

    @pl.when(j == 0)
    def _():
        m_scr[...] = jnp.full(m_scr.shape, NEG_BIG, F32)
        l_scr[...] = jnp.zeros(l_scr.shape, F32)
        acc_scr[...] = jnp.zeros(acc_scr.shape, F32)

    bias = bias_ref[0].astype(F32)

    def logits(h):
        hs = slice(h * HEAD_DIM, (h + 1) * HEAD_DIM)
        return _dot_nt(k_ref[0, :, hs], q_ref[0, :, hs])

    pending = [logits(h) for h in range(ATT_LOOKAHEAD)]
    for h in range(N_HEADS):
        hs = slice(h * HEAD_DIM, (h + 1) * HEAD_DIM)
        raw = pending.pop(0)
        if h + ATT_LOOKAHEAD < N_HEADS:
            pending.append(logits(h + ATT_LOOKAHEAD))
        s = raw * log2_scale + bias
        m_prev = m_scr[h]
        m_new = jnp.maximum(m_prev, jnp.max(s, axis=0, keepdims=True))
        alpha = jnp.exp2(m_prev - m_new)
        p = jnp.exp2(s - m_new[0:1, :])
        l_scr[h] = alpha * l_scr[h] + jnp.sum(p, axis=0, keepdims=True)
        pv = _dot(vt_ref[0, hs, :], p.astype(BF16))
        acc_scr[hs, :] = alpha[0:1, :] * acc_scr[hs, :] + pv
        m_scr[h] = m_new

    @pl.when(j == n_kblocks - 1)
    def _():
        for h in range(N_HEADS):
            hs = slice(h * HEAD_DIM, (h + 1) * HEAD_DIM)
            out_t = acc_scr[hs, :] / l_scr[h][0:1, :]
            o_ref[0, :, hs] = out_t.T.astype(BF16)


def _run_attn(q, k, v_t, bias, B, L):
    tq = min(ATT_TQ, L)
    tk = min(ATT_TK, L)

    pairs = [(i, j) for i in range(L // tq) for j in range(((i + 1) * tq + tk - 1) // tk)]
    qblk = jnp.asarray([p[0] for p in pairs], I32)
    kblk = jnp.asarray([p[1] for p in pairs], I32)

    grid_spec = pltpu.PrefetchScalarGridSpec(
        num_scalar_prefetch=2,
        grid=(B, len(pairs)),
        in_specs=[
            pl.BlockSpec((1, tq, D_MODEL), lambda b, s, qb, kb: (b, qb[s], 0)),
            pl.BlockSpec((1, tk, D_MODEL), lambda b, s, qb, kb: (b, kb[s], 0)),
            pl.BlockSpec((1, D_MODEL, tk), lambda b, s, qb, kb: (b, 0, kb[s])),
            pl.BlockSpec((1, tk, tq), lambda b, s, qb, kb: (b, kb[s], qb[s])),
        ],
        out_specs=pl.BlockSpec((1, tq, D_MODEL), lambda b, s, qb, kb: (b, qb[s], 0)),
        scratch_shapes=[
            pltpu.VMEM((N_HEADS, 8, tq), F32),
            pltpu.VMEM((N_HEADS, 8, tq), F32),
            pltpu.VMEM((D_MODEL, tq), F32),
        ],
    )
    return pl.pallas_call(
        _attn_kernel,
        grid_spec=grid_spec,
        out_shape=jax.ShapeDtypeStruct((B, L, D_MODEL), BF16),
        compiler_params=pltpu.CompilerParams(
            dimension_semantics=("arbitrary", "arbitrary"), vmem_limit_bytes=VMEM_LIMIT),
        name="attention",
    )(qblk, kblk, q, k, v_t, bias)


def _post_kernel(mixa_ref, sgb_ref, bo_ref, x_ref, wbb_ref, wo_ref, g_ref, b_ref, h_ref):
    tm = h_ref.shape[0]
    subs = [slice(r, r + POST_SUB) for r in range(0, tm, POST_SUB)]
    branch_b = [_dot(bo_ref[rows, :], wbb_ref[...]) for rows in subs]
    ys = []
    for rows, bb in zip(subs, branch_b):
        mix = mixa_ref[rows, :] + sgb_ref[rows, :] * bb
        ys.append(_dot(mix.astype(BF16), wo_ref[...]))
    for rows, y in zip(subs, ys):
        h_ref[rows, :] = _layer_norm(ALPHA * x_ref[rows, :] + y, g_ref[...], b_ref[...])


def _run_post(mixa, sgb, b_out, x2, wbb, wo, g, b):
    M = x2.shape[0]
    tm = POST_TM
    row = lambda i: (i, 0)
    tile = pl.BlockSpec((tm, D_MODEL), row)
    return pl.pallas_call(
        _post_kernel,
        grid=(M // tm,),
        in_specs=[tile, tile, tile, tile, _const_spec(wbb.shape), _const_spec(wo.shape),
                  _const_spec(g.shape), _const_spec(b.shape)],
        out_specs=tile,
        out_shape=jax.ShapeDtypeStruct((M, D_MODEL), F32),
        compiler_params=pltpu.CompilerParams(
            dimension_semantics=("arbitrary",), vmem_limit_bytes=VMEM_LIMIT),
        name="post",
    )(mixa, sgb, b_out, x2, wbb, wo, g, b)


def _ffn_kernel(h_ref, halo_ref, wup_ref, cw_ref, cb_ref, wdn_ref, g_ref, b_ref, o_ref, act_scr,
                *, tiles_per_seq):
    tm = h_ref.shape[0]
    i = pl.program_id(0)
    h = h_ref[...]
    keep = (i % tiles_per_seq != 0).astype(F32)
    xe = jnp.concatenate([halo_ref[...] * keep, h], axis=0).astype(BF16)

    def conv(up, col0):
        cols = slice(col0, col0 + FFN_TF)
        w = cw_ref[:, cols]
        y = cb_ref[:, cols] + w[2:3, :] * up[HALO:, :]
        y = y + w[1:2, :] * up[HALO - 1:HALO - 1 + tm, :]
        return y + w[0:1, :] * up[HALO - 2:HALO - 2 + tm, :]

    def up_proj(c):
        g0 = c * FFN_TF
        v0 = D_FF + c * FFN_TF
        return _dot(xe, wup_ref[:, g0:g0 + FFN_TF]), _dot(xe, wup_ref[:, v0:v0 + FFN_TF])

    n_chunks = D_FF // FFN_TF
    up_next = up_proj(0)
    for c in range(n_chunks):
        g0 = c * FFN_TF
        gate_raw, val_raw = up_next
        if c + 1 < n_chunks:
            up_next = up_proj(c + 1)
        gate = conv(gate_raw, g0)
        val = conv(val_raw, D_FF + g0)
        act_scr[:, g0:g0 + FFN_TF] = (gate * jax.nn.sigmoid(gate) * val).astype(BF16)
    acc = _dot(act_scr[...], wdn_ref[...])
    o_ref[...] = _layer_norm(ALPHA * h + acc, g_ref[...], b_ref[...])


def _run_ffn(h1, wup, cw, cb, wdn, g, b, L):
    M = h1.shape[0]
    tm = min(FFN_TM, L)
    kernel = functools.partial(_ffn_kernel, tiles_per_seq=L // tm)
    halo_blocks = tm // HALO
    return pl.pallas_call(
        kernel,
        grid=(M // tm,),
        in_specs=[
            pl.BlockSpec((tm, D_MODEL), lambda i: (i, 0)),
            pl.BlockSpec((HALO, D_MODEL), lambda i: (jnp.maximum(i * halo_blocks - 1, 0), 0)),
            _const_spec(wup.shape), _const_spec(cw.shape), _const_spec(cb.shape),
            _const_spec(wdn.shape), _const_spec(g.shape), _const_spec(b.shape),
        ],
        out_specs=pl.BlockSpec((tm, D_MODEL), lambda i: (i, 0)),
        out_shape=jax.ShapeDtypeStruct((M, D_MODEL), F32),
        scratch_shapes=[pltpu.VMEM((tm, D_FF), BF16)],
        compiler_params=pltpu.CompilerParams(
            dimension_semantics=("arbitrary",), vmem_limit_bytes=VMEM_LIMIT),
        name="ffn",
    )(h1, h1, wup, cw, cb, wdn, g, b)


def _prep_proj_weights(w_in, ln_v_g, ln_v_b, w_spatial, b_spatial, ln_kidx_g, ln_kidx_b, w_branch_a):
    o = 0
    segs = []
    for n in (D_A, D_A, D_MODEL, D_MODEL, D_MODEL, IDX_HEADS * IDX_DIM, IDX_DIM, IDX_HEADS,
              D_MODEL, D_MODEL):
        segs.append(w_in[:, o:o + n])
        o += n
    wu, wv, wq, wk, wvv, wqi, wki, wwi, wga, wgb = segs
    kin = w_in.shape[0]
    wqi_pad = jnp.pad(wqi.reshape(kin, IDX_HEADS, IDX_DIM),
                      ((0, 0), (0, 0), (0, IDX_PAD - IDX_DIM))).reshape(kin, IDX_HEADS * IDX_PAD)
    wki_pad = jnp.pad(wki, ((0, 0), (0, IDX_PAD - IDX_DIM)))
    wi = jnp.concatenate([wqi_pad, wki_pad], axis=1)
    wwt = jnp.pad(wwi.T, ((0, 16 - IDX_HEADS), (0, 0)))
    causal = jnp.tril(jnp.ones((CHUNK, CHUNK), dtype=bool))
    ws = jnp.where(causal[None], w_spatial, 0)
    bs = jnp.broadcast_to(b_spatial[:, :, None], (G_A, CHUNK, D_A // G_A))
    pad_k = lambda a: jnp.pad(a, (0, IDX_PAD - IDX_DIM)).reshape(1, IDX_PAD)
    bf = lambda a: a.astype(BF16)
    return (bf(wu), bf(wv), bf(wq), bf(wk), bf(wvv.T), bf(wi), bf(wwt), bf(wga), bf(wgb),
            bf(w_branch_a), bf(ws), bs.astype(F32),
            ln_v_g.reshape(1, -1), ln_v_b.reshape(1, -1), pad_k(ln_kidx_g), pad_k(ln_kidx_b))


def _layer(x, w_in, ln_v_g, ln_v_b, w_spatial, b_spatial, ln_kidx_g, ln_kidx_b, w_branch_a,
           w_branch_b, w_o, ln1_g, ln1_b, w_up, conv_w, conv_b, w_down, ln2_g, ln2_b):
    B, L, D = x.shape
    M = B * L
    topk = min(TOPK_MAX, L // 4)
    x2 = x.reshape(M, D)
    pw = _prep_proj_weights(w_in, ln_v_g, ln_v_b, w_spatial, b_spatial, ln_kidx_g, ln_kidx_b,
                            w_branch_a)
    mixa, q, k, v_t, q_idx, k_idx, w_idx_t, sgb = _run_proj(x2, pw, B, L)
    bias = _run_index(q_idx.reshape(B, L, -1), k_idx.reshape(B, L, -1), w_idx_t, B, L, topk)
    b_out = _run_attn(q.reshape(B, L, D), k.reshape(B, L, D), v_t, bias, B, L)
    h1 = _run_post(mixa, sgb, b_out.reshape(M, D), x2, w_branch_b.astype(BF16), w_o.astype(BF16),
                   ln1_g.reshape(1, -1), ln1_b.reshape(1, -1))
    h2 = _run_ffn(h1, w_up.astype(BF16), conv_w, conv_b.reshape(1, -1), w_down.astype(BF16),
                  ln2_g.reshape(1, -1), ln2_b.reshape(1, -1), L)
    return h2.reshape(B, L, D)


def kernel(x, w_in, ln_v_g, ln_v_b, w_spatial, b_spatial, ln_kidx_g, ln_kidx_b, w_branch_a,
           w_branch_b, w_o, ln1_g, ln1_b, w_up, conv_w, conv_b, w_down, ln2_g, ln2_b):
    h = x
    for i in range(w_in.shape[0]):
        h = _layer(h, w_in[i], ln_v_g[i], ln_v_b[i], w_spatial[i], b_spatial[i], ln_kidx_g[i],
                   ln_kidx_b[i], w_branch_a[i], w_branch_b[i], w_o[i], ln1_g[i], ln1_b[i],
                   w_up[i], conv_w[i], conv_b[i], w_down[i], ln2_g[i], ln2_b[i])
    return h
```
